```python
import jax, jax.numpy as jnp
from jax import lax
import numpy as np

D_MODEL = 1024
BATCH = 1
SEQ = 16384
DEPTH = 4

N_A_LAYERS = DEPTH // 2
N_B_LAYERS = DEPTH - N_A_LAYERS
CONV_WIDTH = 3
D_FF = ((8 * D_MODEL // 3 + 127) // 128) * 128
HEAD_DIM = 128
N_HEADS = D_MODEL // HEAD_DIM
BLOCK_SIZE = 256
TOP_K_BLOCKS = 3
Q_CHUNK = 128
EPS = 1e-6

kernel_name = "yoco_shortconv_moba_convffn"


def rmsnorm(x, g):
    xf = x.astype(jnp.float32)
    y = xf * lax.rsqrt(jnp.mean(xf * xf, axis=-1, keepdims=True) + EPS)
    return (y * g.astype(jnp.float32)).astype(x.dtype)


def causal_dwconv(x, w):
    c = x.shape[-1]
    return lax.conv_general_dilated(
        x, w[:, None, :].astype(x.dtype), window_strides=(1,),
        padding=[(CONV_WIDTH - 1, 0)], dimension_numbers=("NWC", "WIO", "NWC"),
        feature_group_count=c)


def short_conv_mixer(h, w_in, w_conv, w_out):
    bg, cg, xv = jnp.split(h @ w_in, 3, axis=-1)
    return (bg * causal_dwconv(cg * xv, w_conv)) @ w_out


def conv_ffn(h, w_up, w_conv, b_conv, w_down):
    g, u = jnp.split(h @ w_up, 2, axis=-1)
    g = causal_dwconv(g, w_conv) + b_conv.astype(h.dtype)
    return (jax.nn.silu(g) * u) @ w_down


def alibi_slopes(n_heads):
    return jnp.asarray([2.0 ** (-8.0 * (i + 1) / n_heads) for i in range(n_heads)], dtype=jnp.float32)


def shared_kv(x, kv_norm, w_kv):
    b, s, _ = x.shape
    h = rmsnorm(x, kv_norm)
    k, v = jnp.split(h @ w_kv, 2, axis=-1)
    n_blocks = -(-s // BLOCK_SIZE)
    pad = n_blocks * BLOCK_SIZE - s
    def to_blocks(t):
        t = t.reshape(b, s, N_HEADS, HEAD_DIM).transpose(0, 2, 1, 3)
        t = jnp.pad(t, ((0, 0), (0, 0), (0, pad), (0, 0)))
        return t.reshape(b, N_HEADS, n_blocks, BLOCK_SIZE, HEAD_DIM)
    k_blocks, v_blocks = to_blocks(k), to_blocks(v)
    k_mean = jnp.mean(k_blocks.astype(jnp.float32), axis=3)
    return k_blocks, v_blocks, k_mean


def moba_attention(q, k_blocks, v_blocks, k_mean):
    b, h, s, dh = q.shape
    n_blocks = k_blocks.shape[2]
    n_chunks = s // Q_CHUNK
    k_eff = min(TOP_K_BLOCKS, n_blocks)
    scale = dh ** -0.5
    slopes = alibi_slopes(h)[None, :, None, None]
    bi = jnp.arange(b)[:, None, None, None]
    hi = jnp.arange(h)[None, :, None, None]
    offs = jnp.arange(BLOCK_SIZE, dtype=jnp.int32)
    kbs = k_eff * BLOCK_SIZE

    def chunk(c):
        start = c * Q_CHUNK
        q_c = lax.dynamic_slice_in_dim(q, start, Q_CHUNK, axis=2)
        pos_q = start + jnp.arange(Q_CHUNK, dtype=jnp.int32)
        own = start // BLOCK_SIZE
        gate = jnp.einsum("bhqd,bhnd->bhqn", q_c.astype(jnp.float32), k_mean)
        gate = jnp.where(jnp.arange(n_blocks) < own, gate, -jnp.inf)
        _, idx = lax.top_k(gate, k_eff)
        sel_valid = idx < own
        k_sel = k_blocks[bi, hi, idx]
        v_sel = v_blocks[bi, hi, idx]
        s_sel = jnp.einsum("bhqd,bhqjsd->bhqjs", q_c, k_sel).astype(jnp.float32) * scale
        dist_sel = (pos_q[None, None, :, None, None] - (idx[..., None] * BLOCK_SIZE + offs)).astype(jnp.float32)
        s_sel = jnp.where(sel_valid[..., None], s_sel - slopes[..., None] * dist_sel, -jnp.inf)
        k_own = lax.dynamic_index_in_dim(k_blocks, own, axis=2, keepdims=False)
        v_own = lax.dynamic_index_in_dim(v_blocks, own, axis=2, keepdims=False)
        s_own = jnp.einsum("bhqd,bhsd->bhqs", q_c, k_own).astype(jnp.float32) * scale
        dist_own = pos_q[:, None] - (own * BLOCK_SIZE + offs)[None, :]
        s_own = jnp.where(dist_own >= 0, s_own - slopes * dist_own.astype(jnp.float32), -jnp.inf)
        logits = jnp.concatenate([s_sel.reshape(b, h, Q_CHUNK, kbs), s_own], axis=-1)
        p = jax.nn.softmax(logits, axis=-1)
        p_sel = p[..., :kbs].reshape(b, h, Q_CHUNK, k_eff, BLOCK_SIZE).astype(q.dtype)
        p_own = p[..., kbs:].astype(q.dtype)
        return (jnp.einsum("bhqjs,bhqjsd->bhqd", p_sel, v_sel)
                + jnp.einsum("bhqs,bhsd->bhqd", p_own, v_own))

    outs = lax.map(chunk, jnp.arange(n_chunks, dtype=jnp.int32))
    return outs.transpose(1, 2, 0, 3, 4).reshape(b, h, s, dh)


def setup_inputs(seed: int = 0) -> dict:
    key = jax.random.key(seed)
    ks = jax.random.split(key, 16)
    D, F = D_MODEL, D_FF
    def nrm(k, shape, fan_in):
        return jax.random.normal(k, shape, jnp.float32) * (fan_in ** -0.5)
    def gain(k, shape):
        return 1.0 + 0.05 * jax.random.normal(k, shape, jnp.float32)
    return {
        "x": jax.random.normal(ks[0], (BATCH, SEQ, D), jnp.float32),
        "norm_mix": gain(ks[1], (DEPTH, D)),
        "norm_ffn": gain(ks[2], (DEPTH, D)),
        "ffn_w_up": nrm(ks[3], (DEPTH, D, 2 * F), D),
        "ffn_conv": nrm(ks[4], (DEPTH, CONV_WIDTH, F), CONV_WIDTH),
        "ffn_conv_b": 0.01 * jax.random.normal(ks[5], (DEPTH, F), jnp.float32),
        "ffn_w_down": nrm(ks[6], (DEPTH, F, D), F),
        "a_w_in": nrm(ks[7], (N_A_LAYERS, D, 3 * D), D),
        "a_conv": nrm(ks[8], (N_A_LAYERS, CONV_WIDTH, D), CONV_WIDTH),
        "a_w_out": nrm(ks[9], (N_A_LAYERS, D, D), D),
        "kv_norm": gain(ks[10], (D,)),
        "w_kv": nrm(ks[11], (D, 2 * D), D),
        "b_w_q": nrm(ks[12], (N_B_LAYERS, D, D), D),
        "b_w_o": nrm(ks[13], (N_B_LAYERS, D, D), D),
        "final_norm": gain(ks[14], (D,)),
    }


def reference(x, norm_mix, norm_ffn, ffn_w_up, ffn_conv, ffn_conv_b, ffn_w_down,
              a_w_in, a_conv, a_w_out, kv_norm, w_kv, b_w_q, b_w_o, final_norm):
    b, s, d = x.shape
    k_blocks = v_blocks = k_mean = None
    for i in range(DEPTH):
        h = rmsnorm(x, norm_mix[i])
        if i < N_A_LAYERS:
            x = x + short_conv_mixer(h, a_w_in[i], a_conv[i], a_w_out[i])
        else:
            j = i - N_A_LAYERS
            q = (h @ b_w_q[j]).reshape(b, s, N_HEADS, HEAD_DIM).transpose(0, 2, 1, 3)
            o = moba_attention(q, k_blocks, v_blocks, k_mean)
            x = x + o.transpose(0, 2, 1, 3).reshape(b, s, d) @ b_w_o[j]
        x = x + conv_ffn(rmsnorm(x, norm_ffn[i]), ffn_w_up[i], ffn_conv[i], ffn_conv_b[i], ffn_w_down[i])
        if i == N_A_LAYERS - 1:
            k_blocks, v_blocks, k_mean = shared_kv(x, kv_norm, w_kv)
    return rmsnorm(x, final_norm)
```

```python
import functools

import jax
import jax.numpy as jnp
from jax import lax
from jax.experimental import pallas as pl
from jax.experimental.pallas import tpu as pltpu

HEAD_DIM = 128
BLOCK_SIZE = 256
TOP_K_BLOCKS = 3
CONV_WIDTH = 3
EPS = 1e-6
MASK_VALUE = -1e30

LANES = 128
SUBLANES = 8
MXU_WIDTH = 256
VMEM_LIMIT_BYTES = 56 * 1024 * 1024

ROW_TILE = 512
KV_ROW_TILE = 2048
Q_TILE = 1024
KV_BLOCKS_PER_STEP = 4
AUG = 2 * HEAD_DIM

_BF16 = jnp.bfloat16
_F32 = jnp.float32


def _resident(shape):
    nd = len(shape)
    return pl.BlockSpec(shape, lambda *_: (0,) * nd, pipeline_mode=pl.Buffered(1))


def _params(n_axes):
    return pltpu.CompilerParams(
        dimension_semantics=("arbitrary",) * n_axes,
        vmem_limit_bytes=VMEM_LIMIT_BYTES)


def _rmsnorm(x, g):
    ms = jnp.mean(x * x, axis=-1, keepdims=True)
    return x * lax.rsqrt(ms + EPS) * g


def _causal_conv3(v, prev, w):
    rows = lax.broadcasted_iota(jnp.int32, v.shape, 0)
    p1 = prev[SUBLANES - 1:SUBLANES, :]
    p2 = prev[SUBLANES - 2:SUBLANES - 1, :]
    v1 = jnp.where(rows == 0, p1, pltpu.roll(v, 1, 0))
    v2 = jnp.where(rows == 0, p2, jnp.where(rows == 1, p1, pltpu.roll(v, 2, 0)))
    return w[0:1, :] * v2 + w[1:2, :] * v1 + w[2:3, :] * v


def _mixer_a_kernel(x_ref, g_ref, win_ref, cw_ref, wout_ref, o_ref, carry_ref):
    d = x_ref.shape[1]
    tm = x_ref.shape[0]

    @pl.when(pl.program_id(0) == 0)
    def _():
        carry_ref[...] = jnp.zeros_like(carry_ref)

    x = x_ref[...]
    h = _rmsnorm(x, g_ref[...]).astype(_BF16)
    acc = x
    for c in range(d // MXU_WIDTH):
        cs = slice(c * MXU_WIDTH, (c + 1) * MXU_WIDTH)
        bg = jnp.dot(h, win_ref[:, cs], preferred_element_type=_F32)
        cg = jnp.dot(h, win_ref[:, d + c * MXU_WIDTH:d + (c + 1) * MXU_WIDTH],
                     preferred_element_type=_F32)
        xv = jnp.dot(h, win_ref[:, 2 * d + c * MXU_WIDTH:2 * d + (c + 1) * MXU_WIDTH],
                     preferred_element_type=_F32)
        u = cg * xv
        y = bg * _causal_conv3(u, carry_ref[:, cs], cw_ref[:, cs])
        carry_ref[:, cs] = u[tm - SUBLANES:, :]
        acc = acc + jnp.dot(y.astype(_BF16), wout_ref[cs, :], preferred_element_type=_F32)
    o_ref[...] = acc


def _mixer_a(x, g, w_in, conv_w, w_out):
    s, d = x.shape
    tm = min(ROW_TILE, s)
    return pl.pallas_call(
        _mixer_a_kernel,
        grid=(s // tm,),
        in_specs=[
            pl.BlockSpec((tm, d), lambda i: (i, 0)),
            _resident((1, d)),
            _resident(w_in.shape),
            _resident(conv_w.shape),
            _resident(w_out.shape),
        ],
        out_specs=pl.BlockSpec((tm, d), lambda i: (i, 0)),
        out_shape=jax.ShapeDtypeStruct((s, d), _F32),
        scratch_shapes=[pltpu.VMEM((SUBLANES, d), _F32)],
        compiler_params=_params(1),
        name="mixer_a",
    )(x, g, w_in, conv_w, w_out)


def _ffn_kernel(*refs, has_proj, has_final):
    refs = list(refs)
    x_ref = refs.pop(0)
    if has_proj:
        attn_ref = refs.pop(0)
        wo_ref = refs.pop(0)
    g_ref, wup_ref, cw_ref, cb_ref, wdn_ref = refs[:5]
    refs = refs[5:]
    if has_final:
        gf_ref = refs.pop(0)
    o_ref, carry_ref = refs

    tm = x_ref.shape[0]
    f = wdn_ref.shape[0]

    @pl.when(pl.program_id(0) == 0)
    def _():
        carry_ref[...] = jnp.zeros_like(carry_ref)

    x = x_ref[...]
    if has_proj:
        x = x + jnp.dot(attn_ref[...], wo_ref[...], preferred_element_type=_F32)
    h = _rmsnorm(x, g_ref[...]).astype(_BF16)
    acc = x
    for c in range(f // MXU_WIDTH):
        cs = slice(c * MXU_WIDTH, (c + 1) * MXU_WIDTH)
        gp = jnp.dot(h, wup_ref[:, cs], preferred_element_type=_F32)
        up = jnp.dot(h, wup_ref[:, f + c * MXU_WIDTH:f + (c + 1) * MXU_WIDTH],
                     preferred_element_type=_F32)
        gc = _causal_conv3(gp, carry_ref[:, cs], cw_ref[:, cs]) + cb_ref[:, cs]
        carry_ref[:, cs] = gp[tm - SUBLANES:, :]
        act = gc * jax.nn.sigmoid(gc) * up
        acc = acc + jnp.dot(act.astype(_BF16), wdn_ref[cs, :], preferred_element_type=_F32)
    if has_final:
        acc = _rmsnorm(acc, gf_ref[...])
    o_ref[...] = acc


def _ffn(x, g, w_up, conv_w, conv_b, w_down, attn=None, w_o=None, g_final=None):
    s, d = x.shape
    f = w_down.shape[0]
    tm = min(ROW_TILE, s)
    row_spec = pl.BlockSpec((tm, d), lambda i: (i, 0))
    args, specs = [x], [row_spec]
    if attn is not None:
        args += [attn, w_o]
        specs += [row_spec, _resident(w_o.shape)]
    args += [g, w_up, conv_w, conv_b, w_down]
    specs += [_resident((1, d)), _resident(w_up.shape), _resident(conv_w.shape),
              _resident((1, f)), _resident(w_down.shape)]
    if g_final is not None:
        args.append(g_final)
        specs.append(_resident((1, d)))
    return pl.pallas_call(
        functools.partial(_ffn_kernel, has_proj=attn is not None,
                          has_final=g_final is not None),
        grid=(s // tm,),
        in_specs=specs,
        out_specs=row_spec,
        out_shape=jax.ShapeDtypeStruct((s, d), _F32),
        scratch_shapes=[pltpu.VMEM((SUBLANES, f), _F32)],
        compiler_params=_params(1),
        name="conv_ffn",
    )(*args)


def _kv_kernel(x_ref, g_ref, wkv_ref, kt_ref, v_ref, km_ref):
    d = x_ref.shape[1]
    nb = kt_ref.shape[0]
    h = _rmsnorm(x_ref[...], g_ref[...]).astype(_BF16)
    k = jnp.dot(h, wkv_ref[:, :d], preferred_element_type=_F32)
    v = jnp.dot(h, wkv_ref[:, d:], preferred_element_type=_F32)
    v_ref[...] = v.astype(_BF16)
    means = []
    for b in range(nb):
        kb = k[b * BLOCK_SIZE:(b + 1) * BLOCK_SIZE, :]
        means.append(jnp.mean(kb, axis=0, keepdims=True))
        kt_ref[b] = kb.T.astype(_BF16)
    km_ref[...] = jnp.concatenate(means, axis=0)


def _shared_kv(x, g, w_kv):
    s, d = x.shape
    tm = min(KV_ROW_TILE, s)
    nb_step = tm // BLOCK_SIZE
    nb = s // BLOCK_SIZE
    return pl.pallas_call(
        _kv_kernel,
        grid=(s // tm,),
        in_specs=[
            pl.BlockSpec((tm, d), lambda i: (i, 0)),
            _resident((1, d)),
            _resident(w_kv.shape),
        ],
        out_specs=[
            pl.BlockSpec((nb_step, d, BLOCK_SIZE), lambda i: (i, 0, 0)),
            pl.BlockSpec((tm, d), lambda i: (i, 0)),
            pl.BlockSpec((nb_step, d), lambda i: (i, 0)),
        ],
        out_shape=[
            jax.ShapeDtypeStruct((nb, d, BLOCK_SIZE), _BF16),
            jax.ShapeDtypeStruct((s, d), _BF16),
            jax.ShapeDtypeStruct((nb, d), _F32),
        ],
        compiler_params=_params(1),
        name="shared_kv",
    )(x, g, w_kv)


def _q_gate_kernel(x_ref, g_ref, wq_ref, kmt_ref, o_ref):
    tm, d = x_ref.shape
    n_heads = d // HEAD_DIM
    scale = HEAD_DIM ** -0.5
    h = _rmsnorm(x_ref[...], g_ref[...]).astype(_BF16)
    q = jnp.dot(h, wq_ref[...], preferred_element_type=_F32)

    pos = pl.program_id(0) * tm + lax.broadcasted_iota(jnp.int32, (tm, 1), 0)
    own = lax.shift_right_logical(pos, BLOCK_SIZE.bit_length() - 1)
    blk = lax.broadcasted_iota(jnp.int32, (tm, LANES), 1)
    valid = blk < own
    rel = (blk - own).astype(_F32)

    for hd in range(n_heads):
        qh = q[:, hd * HEAD_DIM:(hd + 1) * HEAD_DIM]
        gate = jnp.dot(qh, kmt_ref[hd], preferred_element_type=_F32,
                       precision=lax.Precision.HIGHEST)
        gate = jnp.where(valid, gate, -jnp.inf)
        keep = blk == own
        for _ in range(TOP_K_BLOCKS):
            m = jnp.max(gate, axis=-1, keepdims=True)
            first = jnp.min(jnp.where(gate == m, blk, LANES), axis=-1, keepdims=True)
            pick = (blk == first) & (m > -jnp.inf)
            keep = keep | pick
            gate = jnp.where(pick, -jnp.inf, gate)
        slope = 2.0 ** (-8.0 * (hd + 1) / n_heads)
        bias = jnp.where(keep, (slope * BLOCK_SIZE) * rel, MASK_VALUE)
        n_blk_lanes = LANES // 2
        aux = jnp.where(blk < n_blk_lanes, bias,
                        jnp.where(blk == n_blk_lanes, slope, 0.0))
        o_ref[:, hd * AUG:hd * AUG + HEAD_DIM] = (qh * scale).astype(_BF16)
        o_ref[:, hd * AUG + HEAD_DIM:(hd + 1) * AUG] = aux.astype(_BF16)


def _q_gate(x, g, w_q, kmt):
    s, d = x.shape
    n_heads = d // HEAD_DIM
    tm = min(ROW_TILE, s)
    return pl.pallas_call(
        _q_gate_kernel,
        grid=(s // tm,),
        in_specs=[
            pl.BlockSpec((tm, d), lambda i: (i, 0)),
            _resident((1, d)),
            _resident(w_q.shape),
            _resident(kmt.shape),
        ],
        out_specs=pl.BlockSpec((tm, n_heads * AUG), lambda i: (i, 0)),
        out_shape=jax.ShapeDtypeStruct((s, n_heads * AUG), _BF16),
        compiler_params=_params(1),
        name="q_gate",
    )(x, g, w_q, kmt)


def _attn_kernel(q_ref, kt_ref, v_ref, o_ref, kaux_ref, m_ref, acc_ref, *, kb):
    tq = q_ref.shape[0]
    nb = kt_ref.shape[0]
    g = tq // BLOCK_SIZE
    i = pl.program_id(1)

    @pl.when((pl.program_id(0) == 0) & (i == 0))
    def _():
        row = lax.broadcasted_iota(jnp.int32, (HEAD_DIM, BLOCK_SIZE), 0)
        off = lax.broadcasted_iota(jnp.int32, (HEAD_DIM, BLOCK_SIZE), 1).astype(_F32)
        base = jnp.where(row == LANES // 2, off, 0.0)
        for n in range(nb):
            kaux_ref[n] = jnp.where(row == n, 1.0, base).astype(_BF16)

    lane = lax.broadcasted_iota(jnp.int32, (BLOCK_SIZE, HEAD_DIM), 1)
    ones_col = jnp.where(lane == 0, 1.0, 0.0).astype(_BF16)

    q = q_ref[...]
    m_ref[...] = jnp.full_like(m_ref, MASK_VALUE)
    acc_ref[...] = jnp.zeros_like(acc_ref)

    def step(blocks, causal):
        s = jnp.concatenate(
            [jnp.dot(q, jnp.concatenate([kt_ref[n], kaux_ref[n]], axis=0),
                     preferred_element_type=_F32) for n in blocks], axis=1)
        if causal:
            qpos = lax.broadcasted_iota(jnp.int32, s.shape, 0)
            kpos = lax.broadcasted_iota(jnp.int32, s.shape, 1)
            s = jnp.where(kpos <= qpos, s, MASK_VALUE)
        m_prev = m_ref[...]
        m_new = jnp.maximum(m_prev, jnp.max(s, axis=-1, keepdims=True))
        alpha = jnp.exp(m_prev - m_new)
        p = jnp.exp(s - m_new).astype(_BF16)
        pv = None
        for j, n in enumerate(blocks):
            v_aug = jnp.concatenate([v_ref[n], ones_col], axis=1)
            t = jnp.dot(p[:, j * BLOCK_SIZE:(j + 1) * BLOCK_SIZE], v_aug,
                        preferred_element_type=_F32)
            pv = t if pv is None else pv + t
        acc_ref[...] = alpha * acc_ref[...] + pv
        m_ref[...] = m_new

    def past_body(t, carry):
        step([t * kb + j for j in range(kb)], causal=False)
        return carry

    lax.fori_loop(0, (i * g) // kb, past_body, 0)
    step([i * g + j for j in range(g)], causal=True)

    acc = acc_ref[...]
    o_ref[...] = (acc[:, :HEAD_DIM] / acc[:, HEAD_DIM:HEAD_DIM + 1]).astype(o_ref.dtype)


def _attention(q_aug, kt, v):
    s = q_aug.shape[0]
    nb, d, _ = kt.shape
    n_heads = d // HEAD_DIM
    tq = min(Q_TILE, s)
    g = tq // BLOCK_SIZE
    kb = min(KV_BLOCKS_PER_STEP, g)
    assert g % kb == 0
    v3 = v.reshape(nb, BLOCK_SIZE, d)
    return pl.pallas_call(
        functools.partial(_attn_kernel, kb=kb),
        grid=(n_heads, s // tq),
        in_specs=[
            pl.BlockSpec((tq, AUG), lambda h, i: (i, h)),
            pl.BlockSpec((nb, HEAD_DIM, BLOCK_SIZE), lambda h, i: (0, h, 0)),
            pl.BlockSpec((nb, BLOCK_SIZE, HEAD_DIM), lambda h, i: (0, 0, h)),
        ],
        out_specs=pl.BlockSpec((tq, HEAD_DIM), lambda h, i: (i, h)),
        out_shape=jax.ShapeDtypeStruct((s, d), _BF16),
        scratch_shapes=[
            pltpu.VMEM((nb, HEAD_DIM, BLOCK_SIZE), _BF16),
            pltpu.VMEM((tq, 1), _F32),
            pltpu.VMEM((tq, AUG), _F32),
        ],
        compiler_params=_params(2),
        name="moba_attention",
    )(q_aug, kt, v3)


def kernel(x, norm_mix, norm_ffn, ffn_w_up, ffn_conv, ffn_conv_b, ffn_w_down,
           a_w_in, a_conv, a_w_out, kv_norm, w_kv, b_w_q, b_w_o, final_norm):
    b, s, d = x.shape
    assert b == 1 and s % BLOCK_SIZE == 0 and d % MXU_WIDTH == 0
    depth = norm_mix.shape[0]
    n_a = a_w_in.shape[0]
    n_heads = d // HEAD_DIM
    nb = s // BLOCK_SIZE
    assert nb <= LANES // 2, "block bias lanes hold at most LANES/2 key blocks"

    bf = lambda w: w.astype(_BF16)
    ffn_w_up, ffn_w_down = bf(ffn_w_up), bf(ffn_w_down)
    a_w_in, a_w_out = bf(a_w_in), bf(a_w_out)
    w_kv, b_w_q, b_w_o = bf(w_kv), bf(b_w_q), bf(b_w_o)

    xs = x.reshape(s, d)
    kt = v = kmt = None
    for i in range(depth):
        gmix = norm_mix[i].reshape(1, d)
        attn = w_o = None
        if i < n_a:
            xs = _mixer_a(xs, gmix, a_w_in[i], a_conv[i], a_w_out[i])
        else:
            j = i - n_a
            attn = _attention(_q_gate(xs, gmix, b_w_q[j], kmt), kt, v)
            w_o = b_w_o[j]
        xs = _ffn(xs, norm_ffn[i].reshape(1, d), ffn_w_up[i], ffn_conv[i],
                  ffn_conv_b[i].reshape(1, -1), ffn_w_down[i], attn=attn, w_o=w_o,
                  g_final=final_norm.reshape(1, d) if i == depth - 1 else None)
        if i == n_a - 1:
            kt, v, km = _shared_kv(xs, kv_norm.reshape(1, d), w_kv)
            kmt = km.reshape(nb, n_heads, HEAD_DIM).transpose(1, 2, 0)
            kmt = jnp.pad(kmt, ((0, 0), (0, 0), (0, LANES - nb)))
    return xs.reshape(b, s, d)
```

```python
import functools

import jax
import jax.numpy as jnp
from jax import lax
from jax.experimental import pallas as pl
from jax.experimental.pallas import tpu as pltpu

HEAD_DIM = 128
BLOCK_SIZE = 256
TOP_K_BLOCKS = 3
CONV_WIDTH = 3
EPS = 1e-6
MASK_VALUE = -1e30

LANES = 128
SUBLANES = 8
MXU_WIDTH = 256
VMEM_LIMIT_BYTES = 56 * 1024 * 1024

ROW_TILE = 512
KV_ROW_TILE = 2048
Q_TILE = 1024
KV_BLOCKS_PER_STEP = 4
AUG = 2 * HEAD_DIM

_BF16 = jnp.bfloat16
_F32 = jnp.float32


def _resident(shape):
    nd = len(shape)
    return pl.BlockSpec(shape, lambda *_: (0,) * nd, pipeline_mode=pl.Buffered(1))


def _params(n_axes):
    return pltpu.CompilerParams(
        dimension_semantics=("arbitrary",) * n_axes,
        vmem_limit_bytes=VMEM_LIMIT_BYTES)


def _rmsnorm(x, g):
    ms = jnp.mean(x * x, axis=-1, keepdims=True)
    return x * lax.rsqrt(ms + EPS) * g


def _causal_conv3(v, prev, w):
    rows = lax.broadcasted_iota(jnp.int32, v.shape, 0)
    p1 = prev[SUBLANES - 1:SUBLANES, :]
    p2 = prev[SUBLANES - 2:SUBLANES - 1, :]
    v1 = jnp.where(rows == 0, p1, pltpu.roll(v, 1, 0))
    v2 = jnp.where(rows == 0, p2, jnp.where(rows == 1, p1, pltpu.roll(v, 2, 0)))
    return w[0:1, :] * v2 + w[1:2, :] * v1 + w[2:3, :] * v


def _mixer_a_kernel(x_ref, g_ref, win_ref, cw_ref, wout_ref, o_ref, carry_ref):
    d = x_ref.shape[1]
    tm = x_ref.shape[0]

    @pl.when(pl.program_id(0) == 0)
    def _():
        carry_ref[...] = jnp.zeros_like(carry_ref)

    x = x_ref[...]
    h = _rmsnorm(x, g_ref[...]).astype(_BF16)
    acc = x
    for c in range(d // MXU_WIDTH):
        cs = slice(c * MXU_WIDTH, (c + 1) * MXU_WIDTH)
        bg = jnp.dot(h, win_ref[:, cs], preferred_element_type=_F32)
        cg = jnp.dot(h, win_ref[:, d + c * MXU_WIDTH:d + (c + 1) * MXU_WIDTH],
                     preferred_element_type=_F32)
        xv = jnp.dot(h, win_ref[:, 2 * d + c * MXU_WIDTH:2 * d + (c + 1) * MXU_WIDTH],
                     preferred_element_type=_F32)
        u = cg * xv
        y = bg * _causal_conv3(u, carry_ref[:, cs], cw_ref[:, cs])
        carry_ref[:, cs] = u[tm - SUBLANES:, :]
        acc = acc + jnp.dot(y.astype(_BF16), wout_ref[cs, :], preferred_element_type=_F32)
    o_ref[...] = acc


def _mixer_a(x, g, w_in, conv_w, w_out):
    s, d = x.shape
    tm = min(ROW_TILE, s)
    return pl.pallas_call(
        _mixer_a_kernel,
        grid=(s // tm,),
        in_specs=[
            pl.BlockSpec((tm, d), lambda i: (i, 0)),
            _resident((1, d)),
            _resident(w_in.shape),
            _resident(conv_w.shape),
            _resident(w_out.shape),
        ],
        out_specs=pl.BlockSpec((tm, d), lambda i: (i, 0)),
        out_shape=jax.ShapeDtypeStruct((s, d), _F32),
        scratch_shapes=[pltpu.VMEM((SUBLANES, d), _F32)],
        compiler_params=_params(1),
        name="mixer_a",
    )(x, g, w_in, conv_w, w_out)


def _ffn_kernel(*refs, has_proj, has_final):
    refs = list(refs)
    x_ref = refs.pop(0)
    if has_proj:
        attn_ref = refs.pop(0)
        wo_ref = refs.pop(0)
    g_ref, wup_ref, cw_ref, cb_ref, wdn_ref = refs[:5]
    refs = refs[5:]
    if has_final:
        gf_ref = refs.pop(0)
    o_ref, carry_ref = refs

    tm = x_ref.shape[0]
    f = wdn_ref.shape[0]

    @pl.when(pl.program_id(0) == 0)
    def _():
        carry_ref[...] = jnp.zeros_like(carry_ref)

    x = x_ref[...]
    if has_proj:
        x = x + jnp.dot(attn_ref[...], wo_ref[...], preferred_element_type=_F32)
    h = _rmsnorm(x, g_ref[...]).astype(_BF16)
    acc = x
    for c in range(f // MXU_WIDTH):
        cs = slice(c * MXU_WIDTH, (c + 1) * MXU_WIDTH)
        gp = jnp.dot(h, wup_ref[:, cs], preferred_element_type=_F32)
        up = jnp.dot(h, wup_ref[:, f + c * MXU_WIDTH:f + (c + 1) * MXU_WIDTH],
                     preferred_element_type=_F32)
        gc = _causal_conv3(gp, carry_ref[:, cs], cw_ref[:, cs]) + cb_ref[:, cs]
        carry_ref[:, cs] = gp[tm - SUBLANES:, :]
        act = gc * jax.nn.sigmoid(gc) * up
        acc = acc + jnp.dot(act.astype(_BF16), wdn_ref[cs, :], preferred_element_type=_F32)
    if has_final:
        acc = _rmsnorm(acc, gf_ref[...])
    o_ref[...] = acc


def _ffn(x, g, w_up, conv_w, conv_b, w_down, attn=None, w_o=None, g_final=None):
    s, d = x.shape
    f = w_down.shape[0]
    tm = min(ROW_TILE, s)
    row_spec = pl.BlockSpec((tm, d), lambda i: (i, 0))
    args, specs = [x], [row_spec]
    if attn is not None:
        args += [attn, w_o]
        specs += [row_spec, _resident(w_o.shape)]
    args += [g, w_up, conv_w, conv_b, w_down]
    specs += [_resident((1, d)), _resident(w_up.shape), _resident(conv_w.shape),
              _resident((1, f)), _resident(w_down.shape)]
    if g_final is not None:
        args.append(g_final)
        specs.append(_resident((1, d)))
    return pl.pallas_call(
        functools.partial(_ffn_kernel, has_proj=attn is not None,
                          has_final=g_final is not None),
        grid=(s // tm,),
        in_specs=specs,
        out_specs=row_spec,
        out_shape=jax.ShapeDtypeStruct((s, d), _F32),
        scratch_shapes=[pltpu.VMEM((SUBLANES, f), _F32)],
        compiler_params=_params(1),
        name="conv_ffn",
    )(*args)


def _kv_kernel(x_ref, g_ref, wkv_ref, kt_ref, v_ref, km_ref):
    d = x_ref.shape[1]
    nb = kt_ref.shape[0]
    h = _rmsnorm(x_ref[...], g_ref[...]).astype(_BF16)
    k = jnp.dot(h, wkv_ref[:, :d], preferred_element_type=_F32)
    v = jnp.dot(h, wkv_ref[:, d:], preferred_element_type=_F32)
    v_ref[...] = v.astype(_BF16)
    means = []
    for b in range(nb):
        kb = k[b * BLOCK_SIZE:(b + 1) * BLOCK_SIZE, :]
        means.append(jnp.mean(kb, axis=0, keepdims=True))
        kt_ref[b] = kb.T.astype(_BF16)
    km_ref[...] = jnp.concatenate(means, axis=0)


def _shared_kv(x, g, w_kv):
    s, d = x.shape
    tm = min(KV_ROW_TILE, s)
    nb_step = tm // BLOCK_SIZE
    nb = s // BLOCK_SIZE
    return pl.pallas_call(
        _kv_kernel,
        grid=(s // tm,),
        in_specs=[
            pl.BlockSpec((tm, d), lambda i: (i, 0)),
            _resident((1, d)),
            _resident(w_kv.shape),
        ],
        out_specs=[
            pl.BlockSpec((nb_step, d, BLOCK_SIZE), lambda i: (i, 0, 0)),
            pl.BlockSpec((tm, d), lambda i: (i, 0)),
            pl.BlockSpec((nb_step, d), lambda i: (i, 0)),
        ],
        out_shape=[
            jax.ShapeDtypeStruct((nb, d, BLOCK_SIZE), _BF16),
            jax.ShapeDtypeStruct((s, d), _BF16),
            jax.ShapeDtypeStruct((nb, d), _F32),
        ],
        compiler_params=_params(1),
        name="shared_kv",
    )(x, g, w_kv)


def _q_gate_kernel(x_ref, g_ref, wq_ref, kmt_ref, o_ref):
    tm, d = x_ref.shape
    n_heads = d // HEAD_DIM
    scale = HEAD_DIM ** -0.5
    h = _rmsnorm(x_ref[...], g_ref[...]).astype(_BF16)
    q = jnp.dot(h, wq_ref[...], preferred_element_type=_F32)

    pos = pl.program_id(0) * tm + lax.broadcasted_iota(jnp.int32, (tm, 1), 0)
    own = lax.shift_right_logical(pos, BLOCK_SIZE.bit_length() - 1)
    blk = lax.broadcasted_iota(jnp.int32, (tm, LANES), 1)
    valid = blk < own
    rel = (blk - own).astype(_F32)

    for hd in range(n_heads):
        qh = q[:, hd * HEAD_DIM:(hd + 1) * HEAD_DIM]
        gate = jnp.dot(qh, kmt_ref[hd], preferred_element_type=_F32,
                       precision=lax.Precision.HIGHEST)
        gate = jnp.where(valid, gate, -jnp.inf)
        keep = blk == own
        for _ in range(TOP_K_BLOCKS):
            m = jnp.max(gate, axis=-1, keepdims=True)
            first = jnp.min(jnp.where(gate == m, blk, LANES), axis=-1, keepdims=True)
            pick = (blk == first) & (m > -jnp.inf)
            keep = keep | pick
            gate = jnp.where(pick, -jnp.inf, gate)
        slope = 2.0 ** (-8.0 * (hd + 1) / n_heads)
        bias = jnp.where(keep, (slope * BLOCK_SIZE) * rel, MASK_VALUE)
        n_blk_lanes = LANES // 2
        aux = jnp.where(blk < n_blk_lanes, bias,
                        jnp.where(blk == n_blk_lanes, slope, 0.0))
        o_ref[:, hd * AUG:hd * AUG + HEAD_DIM] = (qh * scale).astype(_BF16)
        o_ref[:, hd * AUG + HEAD_DIM:(hd + 1) * AUG] = aux.astype(_BF16)


def _q_gate(x, g, w_q, kmt):
    s, d = x.shape
    n_heads = d // HEAD_DIM
    tm = min(ROW_TILE, s)
    return pl.pallas_call(
        _q_gate_kernel,
        grid=(s // tm,),
        in_specs=[
            pl.BlockSpec((tm, d), lambda i: (i, 0)),
            _resident((1, d)),
            _resident(w_q.shape),
            _resident(kmt.shape),
        ],
        out_specs=pl.BlockSpec((tm, n_heads * AUG), lambda i: (i, 0)),
        out_shape=jax.ShapeDtypeStruct((s, n_heads * AUG), _BF16),
        compiler_params=_params(1),
        name="q_gate",
    )(x, g, w_q, kmt)


def _attn_kernel(q_ref, kt_ref, v_ref, o_ref, kaux_ref, m_ref, acc_ref, *, kb):
    tq = q_ref.shape[0]
    nb = kt_ref.shape[0]
    g = tq // BLOCK_SIZE
    i = pl.program_id(1)

    @pl.when((pl.program_id(0) == 0) & (i == 0))
    def _():
        row = lax.broadcasted_iota(jnp.int32, (HEAD_DIM, BLOCK_SIZE), 0)
        off = lax.broadcasted_iota(jnp.int32, (HEAD_DIM, BLOCK_SIZE), 1).astype(_F32)
        base = jnp.where(row == LANES // 2, off, 0.0)
        for n in range(nb):
            kaux_ref[n] = jnp.where(row == n, 1.0, base).astype(_BF16)

    lane = lax.broadcasted_iota(jnp.int32, (BLOCK_SIZE, HEAD_DIM), 1)
    ones_col = jnp.where(lane == 0, 1.0, 0.0).astype(_BF16)

    m_ref[...] = jnp.full_like(m_ref, MASK_VALUE)
    acc_ref[...] = jnp.zeros_like(acc_ref)

    def chain(r, blocks, causal_last):
        rows = slice(r * BLOCK_SIZE, (r + 1) * BLOCK_SIZE)
        q = q_ref[rows, :]
        parts = [jnp.dot(q, jnp.concatenate([kt_ref[n], kaux_ref[n]], axis=0),
                         preferred_element_type=_F32) for n in blocks]
        if causal_last:
            qpos = lax.broadcasted_iota(jnp.int32, (BLOCK_SIZE, BLOCK_SIZE), 0)
            kpos = lax.broadcasted_iota(jnp.int32, (BLOCK_SIZE, BLOCK_SIZE), 1)
            parts[-1] = jnp.where(kpos <= qpos, parts[-1], MASK_VALUE)
        s = parts[0] if len(parts) == 1 else jnp.concatenate(parts, axis=1)
        m_prev = m_ref[rows, :]
        m_new = jnp.maximum(m_prev, jnp.max(s, axis=-1, keepdims=True))
        alpha = jnp.exp(m_prev - m_new)
        p = jnp.exp(s - m_new).astype(_BF16)
        pv = None
        for j, n in enumerate(blocks):
            v_aug = jnp.concatenate([v_ref[n], ones_col], axis=1)
            t = jnp.dot(p[:, j * BLOCK_SIZE:(j + 1) * BLOCK_SIZE], v_aug,
                        preferred_element_type=_F32)
            pv = t if pv is None else pv + t
        acc_ref[rows, :] = alpha * acc_ref[rows, :] + pv
        m_ref[rows, :] = m_new

    def past_body(t, carry):
        for r in range(g):
            chain(r, [t * kb + j for j in range(kb)], causal_last=False)
        return carry

    lax.fori_loop(0, (i * g) // kb, past_body, 0)
    for r in range(g):
        chain(r, [i * g + j for j in range(r + 1)], causal_last=True)

    acc = acc_ref[...]
    o_ref[...] = (acc[:, :HEAD_DIM] / acc[:, HEAD_DIM:HEAD_DIM + 1]).astype(o_ref.dtype)


def _attention(q_aug, kt, v):
    s = q_aug.shape[0]
    nb, d, _ = kt.shape
    n_heads = d // HEAD_DIM
    tq = min(Q_TILE, s)
    g = tq // BLOCK_SIZE
    kb = min(KV_BLOCKS_PER_STEP, g)
    assert g % kb == 0
    v3 = v.reshape(nb, BLOCK_SIZE, d)
    return pl.pallas_call(
        functools.partial(_attn_kernel, kb=kb),
        grid=(n_heads, s // tq),
        in_specs=[
            pl.BlockSpec((tq, AUG), lambda h, i: (i, h)),
            pl.BlockSpec((nb, HEAD_DIM, BLOCK_SIZE), lambda h, i: (0, h, 0)),
            pl.BlockSpec((nb, BLOCK_SIZE, HEAD_DIM), lambda h, i: (0, 0, h)),
        ],
        out_specs=pl.BlockSpec((tq, HEAD_DIM), lambda h, i: (i, h)),
        out_shape=jax.ShapeDtypeStruct((s, d), _BF16),
        scratch_shapes=[
            pltpu.VMEM((nb, HEAD_DIM, BLOCK_SIZE), _BF16),
            pltpu.VMEM((tq, 1), _F32),
            pltpu.VMEM((tq, AUG), _F32),
        ],
        compiler_params=_params(2),
        name="moba_attention",
    )(q_aug, kt, v3)


def kernel(x, norm_mix, norm_ffn, ffn_w_up, ffn_conv, ffn_conv_b, ffn_w_down,
           a_w_in, a_conv, a_w_out, kv_norm, w_kv, b_w_q, b_w_o, final_norm):
    b, s, d = x.shape
    assert b == 1 and s % BLOCK_SIZE == 0 and d % MXU_WIDTH == 0
    depth = norm_mix.shape[0]
    n_a = a_w_in.shape[0]
    n_heads = d // HEAD_DIM
    nb = s // BLOCK_SIZE
    assert nb <= LANES // 2, "block bias lanes hold at most LANES/2 key blocks"

    bf = lambda w: w.astype(_BF16)
    ffn_w_up, ffn_w_down = bf(ffn_w_up), bf(ffn_w_down)
    a_w_in, a_w_out = bf(a_w_in), bf(a_w_out)
    w_kv, b_w_q, b_w_o = bf(w_kv), bf(b_w_q), bf(b_w_o)

    xs = x.reshape(s, d)
    kt = v = kmt = None
    for i in range(depth):
        gmix = norm_mix[i].reshape(1, d)
        attn = w_o = None
        if i < n_a:
            xs = _mixer_a(xs, gmix, a_w_in[i], a_conv[i], a_w_out[i])
        else:
            j = i - n_a
            attn = _attention(_q_gate(xs, gmix, b_w_q[j], kmt), kt, v)
            w_o = b_w_o[j]
        xs = _ffn(xs, norm_ffn[i].reshape(1, d), ffn_w_up[i], ffn_conv[i],
                  ffn_conv_b[i].reshape(1, -1), ffn_w_down[i], attn=attn, w_o=w_o,
                  g_final=final_norm.reshape(1, d) if i == depth - 1 else None)
        if i == n_a - 1:
            kt, v, km = _shared_kv(xs, kv_norm.reshape(1, d), w_kv)
            kmt = km.reshape(nb, n_heads, HEAD_DIM).transpose(1, 2, 0)
            kmt = jnp.pad(kmt, ((0, 0), (0, 0), (0, LANES - nb)))
    return xs.reshape(b, s, d)
```

```python
import functools

import jax
import jax.numpy as jnp
from jax import lax
from jax.experimental import pallas as pl
from jax.experimental.pallas import tpu as pltpu

HEAD_DIM = 128
BLOCK_SIZE = 256
TOP_K_BLOCKS = 3
CONV_WIDTH = 3
EPS = 1e-6
MASK_VALUE = -1e30

LANES = 128
SUBLANES = 8
MXU_WIDTH = 256
VMEM_LIMIT_BYTES = 56 * 1024 * 1024

ROW_TILE = 512
KV_ROW_TILE = 2048
Q_TILE = 1024
KV_BLOCKS_PER_STEP = 4
AUG = 2 * HEAD_DIM

_BF16 = jnp.bfloat16
_F32 = jnp.float32


def _resident(shape):
    nd = len(shape)
    return pl.BlockSpec(shape, lambda *_: (0,) * nd, pipeline_mode=pl.Buffered(1))


def _params(n_axes):
    return pltpu.CompilerParams(
        dimension_semantics=("arbitrary",) * n_axes,
        vmem_limit_bytes=VMEM_LIMIT_BYTES)


def _rmsnorm(x, g):
    ms = jnp.mean(x * x, axis=-1, keepdims=True)
    return x * lax.rsqrt(ms + EPS) * g


def _causal_conv3(v, prev, w):
    rows = lax.broadcasted_iota(jnp.int32, v.shape, 0)
    p1 = prev[SUBLANES - 1:SUBLANES, :]
    p2 = prev[SUBLANES - 2:SUBLANES - 1, :]
    v1 = jnp.where(rows == 0, p1, pltpu.roll(v, 1, 0))
    v2 = jnp.where(rows == 0, p2, jnp.where(rows == 1, p1, pltpu.roll(v, 2, 0)))
    return w[0:1, :] * v2 + w[1:2, :] * v1 + w[2:3, :] * v


def _mixer_a_kernel(x_ref, g_ref, win_ref, cw_ref, wout_ref, o_ref, carry_ref):
    d = x_ref.shape[1]
    tm = x_ref.shape[0]

    @pl.when(pl.program_id(0) == 0)
    def _():
        carry_ref[...] = jnp.zeros_like(carry_ref)

    x = x_ref[...]
    h = _rmsnorm(x, g_ref[...]).astype(_BF16)
    def in_proj(c):
        return [jnp.dot(h, win_ref[:, k * d + c * MXU_WIDTH:k * d + (c + 1) * MXU_WIDTH],
                        preferred_element_type=_F32) for k in range(3)]

    n_chunks = d // MXU_WIDTH
    acc = x
    nxt = in_proj(0)
    for c in range(n_chunks):
        cs = slice(c * MXU_WIDTH, (c + 1) * MXU_WIDTH)
        bg, cg, xv = nxt
        if c + 1 < n_chunks:
            nxt = in_proj(c + 1)
        u = cg * xv
        y = bg * _causal_conv3(u, carry_ref[:, cs], cw_ref[:, cs])
        carry_ref[:, cs] = u[tm - SUBLANES:, :]
        acc = acc + jnp.dot(y.astype(_BF16), wout_ref[cs, :], preferred_element_type=_F32)
    o_ref[...] = acc


def _mixer_a(x, g, w_in, conv_w, w_out):
    s, d = x.shape
    tm = min(ROW_TILE, s)
    return pl.pallas_call(
        _mixer_a_kernel,
        grid=(s // tm,),
        in_specs=[
            pl.BlockSpec((tm, d), lambda i: (i, 0)),
            _resident((1, d)),
            _resident(w_in.shape),
            _resident(conv_w.shape),
            _resident(w_out.shape),
        ],
        out_specs=pl.BlockSpec((tm, d), lambda i: (i, 0)),
        out_shape=jax.ShapeDtypeStruct((s, d), _F32),
        scratch_shapes=[pltpu.VMEM((SUBLANES, d), _F32)],
        compiler_params=_params(1),
        name="mixer_a",
    )(x, g, w_in, conv_w, w_out)


def _ffn_kernel(*refs, has_proj, has_final):
    refs = list(refs)
    x_ref = refs.pop(0)
    if has_proj:
        attn_ref = refs.pop(0)
        wo_ref = refs.pop(0)
    g_ref, wup_ref, cw_ref, cb_ref, wdn_ref = refs[:5]
    refs = refs[5:]
    if has_final:
        gf_ref = refs.pop(0)
    o_ref, carry_ref = refs

    tm = x_ref.shape[0]
    f = wdn_ref.shape[0]

    @pl.when(pl.program_id(0) == 0)
    def _():
        carry_ref[...] = jnp.zeros_like(carry_ref)

    x = x_ref[...]
    if has_proj:
        x = x + jnp.dot(attn_ref[...], wo_ref[...], preferred_element_type=_F32)
    h = _rmsnorm(x, g_ref[...]).astype(_BF16)
    def up_proj(c):
        return [jnp.dot(h, wup_ref[:, k * f + c * MXU_WIDTH:k * f + (c + 1) * MXU_WIDTH],
                        preferred_element_type=_F32) for k in range(2)]

    n_chunks = f // MXU_WIDTH
    acc = x
    nxt = up_proj(0)
    for c in range(n_chunks):
        cs = slice(c * MXU_WIDTH, (c + 1) * MXU_WIDTH)
        gp, up = nxt
        if c + 1 < n_chunks:
            nxt = up_proj(c + 1)
        gc = _causal_conv3(gp, carry_ref[:, cs], cw_ref[:, cs]) + cb_ref[:, cs]
        carry_ref[:, cs] = gp[tm - SUBLANES:, :]
        act = gc * jax.nn.sigmoid(gc) * up
        acc = acc + jnp.dot(act.astype(_BF16), wdn_ref[cs, :], preferred_element_type=_F32)
    if has_final:
        acc = _rmsnorm(acc, gf_ref[...])
    o_ref[...] = acc


def _ffn(x, g, w_up, conv_w, conv_b, w_down, attn=None, w_o=None, g_final=None):
    s, d = x.shape
    f = w_down.shape[0]
    tm = min(ROW_TILE, s)
    row_spec = pl.BlockSpec((tm, d), lambda i: (i, 0))
    args, specs = [x], [row_spec]
    if attn is not None:
        args += [attn, w_o]
        specs += [row_spec, _resident(w_o.shape)]
    args += [g, w_up, conv_w, conv_b, w_down]
    specs += [_resident((1, d)), _resident(w_up.shape), _resident(conv_w.shape),
              _resident((1, f)), _resident(w_down.shape)]
    if g_final is not None:
        args.append(g_final)
        specs.append(_resident((1, d)))
    return pl.pallas_call(
        functools.partial(_ffn_kernel, has_proj=attn is not None,
                          has_final=g_final is not None),
        grid=(s // tm,),
        in_specs=specs,
        out_specs=row_spec,
        out_shape=jax.ShapeDtypeStruct((s, d), _F32),
        scratch_shapes=[pltpu.VMEM((SUBLANES, f), _F32)],
        compiler_params=_params(1),
        name="conv_ffn",
    )(*args)


def _kv_kernel(x_ref, g_ref, wkv_ref, kt_ref, v_ref, km_ref):
    d = x_ref.shape[1]
    nb = kt_ref.shape[0]
    h = _rmsnorm(x_ref[...], g_ref[...]).astype(_BF16)
    k = jnp.dot(h, wkv_ref[:, :d], preferred_element_type=_F32)
    v = jnp.dot(h, wkv_ref[:, d:], preferred_element_type=_F32)
    v_ref[...] = v.astype(_BF16)
    means = []
    for b in range(nb):
        kb = k[b * BLOCK_SIZE:(b + 1) * BLOCK_SIZE, :]
        means.append(jnp.mean(kb, axis=0, keepdims=True))
        kt_ref[b] = kb.T.astype(_BF16)
    km_ref[...] = jnp.concatenate(means, axis=0)


def _shared_kv(x, g, w_kv):
    s, d = x.shape
    tm = min(KV_ROW_TILE, s)
    nb_step = tm // BLOCK_SIZE
    nb = s // BLOCK_SIZE
    return pl.pallas_call(
        _kv_kernel,
        grid=(s // tm,),
        in_specs=[
            pl.BlockSpec((tm, d), lambda i: (i, 0)),
            _resident((1, d)),
            _resident(w_kv.shape),
        ],
        out_specs=[
            pl.BlockSpec((nb_step, d, BLOCK_SIZE), lambda i: (i, 0, 0)),
            pl.BlockSpec((tm, d), lambda i: (i, 0)),
            pl.BlockSpec((nb_step, d), lambda i: (i, 0)),
        ],
        out_shape=[
            jax.ShapeDtypeStruct((nb, d, BLOCK_SIZE), _BF16),
            jax.ShapeDtypeStruct((s, d), _BF16),
            jax.ShapeDtypeStruct((nb, d), _F32),
        ],
        compiler_params=_params(1),
        name="shared_kv",
    )(x, g, w_kv)


def _q_gate_kernel(x_ref, g_ref, wq_ref, kmt_ref, o_ref):
    tm, d = x_ref.shape
    n_heads = d // HEAD_DIM
    scale = HEAD_DIM ** -0.5
    h = _rmsnorm(x_ref[...], g_ref[...]).astype(_BF16)
    q = jnp.dot(h, wq_ref[...], preferred_element_type=_F32)

    pos = pl.program_id(0) * tm + lax.broadcasted_iota(jnp.int32, (tm, 1), 0)
    own = lax.shift_right_logical(pos, BLOCK_SIZE.bit_length() - 1)
    blk = lax.broadcasted_iota(jnp.int32, (tm, LANES), 1)
    valid = blk < own
    rel = (blk - own).astype(_F32)

    for hd in range(n_heads):
        qh = q[:, hd * HEAD_DIM:(hd + 1) * HEAD_DIM]
        gate = jnp.dot(qh, kmt_ref[hd], preferred_element_type=_F32,
                       precision=lax.Precision.HIGHEST)
        gate = jnp.where(valid, gate, -jnp.inf)
        keep = blk == own
        for _ in range(TOP_K_BLOCKS):
            m = jnp.max(gate, axis=-1, keepdims=True)
            first = jnp.min(jnp.where(gate == m, blk, LANES), axis=-1, keepdims=True)
            pick = (blk == first) & (m > -jnp.inf)
            keep = keep | pick
            gate = jnp.where(pick, -jnp.inf, gate)
        slope = 2.0 ** (-8.0 * (hd + 1) / n_heads)
        bias = jnp.where(keep, (slope * BLOCK_SIZE) * rel, MASK_VALUE)
        n_blk_lanes = LANES // 2
        aux = jnp.where(blk < n_blk_lanes, bias,
                        jnp.where(blk == n_blk_lanes, slope, 0.0))
        o_ref[:, hd * AUG:hd * AUG + HEAD_DIM] = (qh * scale).astype(_BF16)
        o_ref[:, hd * AUG + HEAD_DIM:(hd + 1) * AUG] = aux.astype(_BF16)


def _q_gate(x, g, w_q, kmt):
    s, d = x.shape
    n_heads = d // HEAD_DIM
    tm = min(ROW_TILE, s)
    return pl.pallas_call(
        _q_gate_kernel,
        grid=(s // tm,),
        in_specs=[
            pl.BlockSpec((tm, d), lambda i: (i, 0)),
            _resident((1, d)),
            _resident(w_q.shape),
            _resident(kmt.shape),
        ],
        out_specs=pl.BlockSpec((tm, n_heads * AUG), lambda i: (i, 0)),
        out_shape=jax.ShapeDtypeStruct((s, n_heads * AUG), _BF16),
        compiler_params=_params(1),
        name="q_gate",
    )(x, g, w_q, kmt)


def _attn_kernel(q_ref, kt_ref, v_ref, o_ref, kaux_ref, m_ref, alpha_ref, acc_ref, p_ref,
                 *, kb):
    tq = q_ref.shape[0]
    nb = kt_ref.shape[0]
    g = tq // BLOCK_SIZE
    i = pl.program_id(1)

    @pl.when((pl.program_id(0) == 0) & (i == 0))
    def _():
        row = lax.broadcasted_iota(jnp.int32, (HEAD_DIM, BLOCK_SIZE), 0)
        off = lax.broadcasted_iota(jnp.int32, (HEAD_DIM, BLOCK_SIZE), 1).astype(_F32)
        base = jnp.where(row == LANES // 2, off, 0.0)
        for n in range(nb):
            kaux_ref[n] = jnp.where(row == n, 1.0, base).astype(_BF16)

    lane = lax.broadcasted_iota(jnp.int32, (BLOCK_SIZE, HEAD_DIM), 1)
    ones_col = jnp.where(lane == 0, 1.0, 0.0).astype(_BF16)

    m_ref[...] = jnp.full_like(m_ref, MASK_VALUE)
    acc_ref[...] = jnp.zeros_like(acc_ref)

    def scores(r, blocks):
        q = q_ref[r * BLOCK_SIZE:(r + 1) * BLOCK_SIZE, :]
        return [jnp.dot(q, jnp.concatenate([kt_ref[n], kaux_ref[n]], axis=0),
                        preferred_element_type=_F32) for n in blocks]

    def softmax(r, parts, causal_last):
        rows = slice(r * BLOCK_SIZE, (r + 1) * BLOCK_SIZE)
        if causal_last:
            qpos = lax.broadcasted_iota(jnp.int32, (BLOCK_SIZE, BLOCK_SIZE), 0)
            kpos = lax.broadcasted_iota(jnp.int32, (BLOCK_SIZE, BLOCK_SIZE), 1)
            parts[-1] = jnp.where(kpos <= qpos, parts[-1], MASK_VALUE)
        s = parts[0] if len(parts) == 1 else jnp.concatenate(parts, axis=1)
        m_prev = m_ref[rows, :]
        m_new = jnp.maximum(m_prev, jnp.max(s, axis=-1, keepdims=True))
        alpha_ref[rows, :] = jnp.exp(m_prev - m_new)
        p_ref[rows, :s.shape[1]] = jnp.exp(s - m_new).astype(_BF16)
        m_ref[rows, :] = m_new

    def pv(r, blocks):
        rows = slice(r * BLOCK_SIZE, (r + 1) * BLOCK_SIZE)
        out = None
        for j, n in enumerate(blocks):
            v_aug = jnp.concatenate([v_ref[n], ones_col], axis=1)
            t = jnp.dot(p_ref[rows, j * BLOCK_SIZE:(j + 1) * BLOCK_SIZE], v_aug,
                        preferred_element_type=_F32)
            out = t if out is None else out + t
        acc_ref[rows, :] = alpha_ref[rows, :] * acc_ref[rows, :] + out

    lag = g // 2

    def stage(new_of, causal_last, old_of):
        for r in range(g):
            parts = scores(r, new_of(r))
            if r < lag:
                pv(r + g - lag, old_of(r + g - lag))
            softmax(r, parts, causal_last)
        for r in range(g - lag):
            pv(r, new_of(r))

    def past(t):
        return lambda r: [t * kb + j for j in range(kb)]

    diag = lambda r: [i * g + j for j in range(r + 1)]
    n_past = (i * g) // kb

    p_ref[...] = jnp.zeros_like(p_ref)
    alpha_ref[...] = jnp.ones_like(alpha_ref)

    def body(t, carry):
        stage(past(t), False, past(jnp.maximum(t - 1, 0)))
        return carry

    lax.fori_loop(0, n_past, body, 0)
    stage(diag, True, past(jnp.maximum(n_past - 1, 0)))
    for r in range(g - lag, g):
        pv(r, diag(r))

    acc = acc_ref[...]
    o_ref[...] = (acc[:, :HEAD_DIM] / acc[:, HEAD_DIM:HEAD_DIM + 1]).astype(o_ref.dtype)


def _attention(q_aug, kt, v):
    s = q_aug.shape[0]
    nb, d, _ = kt.shape
    n_heads = d // HEAD_DIM
    tq = min(Q_TILE, s)
    g = tq // BLOCK_SIZE
    kb = min(KV_BLOCKS_PER_STEP, g)
    assert g % kb == 0
    v3 = v.reshape(nb, BLOCK_SIZE, d)
    return pl.pallas_call(
        functools.partial(_attn_kernel, kb=kb),
        grid=(n_heads, s // tq),
        in_specs=[
            pl.BlockSpec((tq, AUG), lambda h, i: (i, h)),
            pl.BlockSpec((nb, HEAD_DIM, BLOCK_SIZE), lambda h, i: (0, h, 0)),
            pl.BlockSpec((nb, BLOCK_SIZE, HEAD_DIM), lambda h, i: (0, 0, h)),
        ],
        out_specs=pl.BlockSpec((tq, HEAD_DIM), lambda h, i: (i, h)),
        out_shape=jax.ShapeDtypeStruct((s, d), _BF16),
        scratch_shapes=[
            pltpu.VMEM((nb, HEAD_DIM, BLOCK_SIZE), _BF16),
            pltpu.VMEM((tq, 1), _F32),
            pltpu.VMEM((tq, 1), _F32),
            pltpu.VMEM((tq, AUG), _F32),
            pltpu.VMEM((tq, max(kb, g) * BLOCK_SIZE), _BF16),
        ],
        compiler_params=_params(2),
        name="moba_attention",
    )(q_aug, kt, v3)


def kernel(x, norm_mix, norm_ffn, ffn_w_up, ffn_conv, ffn_conv_b, ffn_w_down,
           a_w_in, a_conv, a_w_out, kv_norm, w_kv, b_w_q, b_w_o, final_norm):
    b, s, d = x.shape
    assert b == 1 and s % BLOCK_SIZE == 0 and d % MXU_WIDTH == 0
    depth = norm_mix.shape[0]
    n_a = a_w_in.shape[0]
    n_heads = d // HEAD_DIM
    nb = s // BLOCK_SIZE
    assert nb <= LANES // 2, "block bias lanes hold at most LANES/2 key blocks"

    bf = lambda w: w.astype(_BF16)
    ffn_w_up, ffn_w_down = bf(ffn_w_up), bf(ffn_w_down)
    a_w_in, a_w_out = bf(a_w_in), bf(a_w_out)
    w_kv, b_w_q, b_w_o = bf(w_kv), bf(b_w_q), bf(b_w_o)

    xs = x.reshape(s, d)
    kt = v = kmt = None
    for i in range(depth):
        gmix = norm_mix[i].reshape(1, d)
        attn = w_o = None
        if i < n_a:
            xs = _mixer_a(xs, gmix, a_w_in[i], a_conv[i], a_w_out[i])
        else:
            j = i - n_a
            attn = _attention(_q_gate(xs, gmix, b_w_q[j], kmt), kt, v)
            w_o = b_w_o[j]
        xs = _ffn(xs, norm_ffn[i].reshape(1, d), ffn_w_up[i], ffn_conv[i],
                  ffn_conv_b[i].reshape(1, -1), ffn_w_down[i], attn=attn, w_o=w_o,
                  g_final=final_norm.reshape(1, d) if i == depth - 1 else None)
        if i == n_a - 1:
            kt, v, km = _shared_kv(xs, kv_norm.reshape(1, d), w_kv)
            kmt = km.reshape(nb, n_heads, HEAD_DIM).transpose(1, 2, 0)
            kmt = jnp.pad(kmt, ((0, 0), (0, 0), (0, LANES - nb)))
    return xs.reshape(b, s, d)
```

```python
import functools

import jax
import jax.numpy as jnp
from jax import lax
from jax.experimental import pallas as pl
from jax.experimental.pallas import tpu as pltpu

HEAD_DIM = 128
BLOCK_SIZE = 256
TOP_K_BLOCKS = 3
CONV_WIDTH = 3
EPS = 1e-6
MASK_VALUE = -1e30

LANES = 128
SUBLANES = 8
MXU_WIDTH = 256
VMEM_LIMIT_BYTES = 56 * 1024 * 1024

ROW_TILE = 512
KV_ROW_TILE = 2048
Q_TILE = 2048
KV_BLOCKS_PER_STEP = 4
AUG = 2 * HEAD_DIM

_BF16 = jnp.bfloat16
_F32 = jnp.float32


def _resident(shape, layer=None):
    nd = len(shape)
    if layer is None:
        return pl.BlockSpec(shape, lambda *_: (0,) * nd, pipeline_mode=pl.Buffered(1))
    return pl.BlockSpec((None,) + tuple(shape[1:]), lambda *_: (layer,) + (0,) * (nd - 1),
                        pipeline_mode=pl.Buffered(1))


def _params(n_axes):
    return pltpu.CompilerParams(
        dimension_semantics=("arbitrary",) * n_axes,
        vmem_limit_bytes=VMEM_LIMIT_BYTES)


def _rmsnorm(x, g):
    ms = jnp.mean(x * x, axis=-1, keepdims=True)
    return x * lax.rsqrt(ms + EPS) * g


def _causal_conv3(v, prev, w):
    rows = lax.broadcasted_iota(jnp.int32, v.shape, 0)
    p1 = prev[SUBLANES - 1:SUBLANES, :]
    p2 = prev[SUBLANES - 2:SUBLANES - 1, :]
    v1 = jnp.where(rows == 0, p1, pltpu.roll(v, 1, 0))
    v2 = jnp.where(rows == 0, p2, jnp.where(rows == 1, p1, pltpu.roll(v, 2, 0)))
    return w[0:1, :] * v2 + w[1:2, :] * v1 + w[2:3, :] * v


def _mixer_a_kernel(x_ref, g_ref, win_ref, cw_ref, wout_ref, o_ref, carry_ref, y_ref):
    d = x_ref.shape[1]
    tm = x_ref.shape[0]

    @pl.when(pl.program_id(0) == 0)
    def _():
        carry_ref[...] = jnp.zeros_like(carry_ref)

    x = x_ref[...]
    h = _rmsnorm(x, g_ref[...]).astype(_BF16)

    def in_proj(c):
        return [jnp.dot(h, win_ref[:, k * d + c * MXU_WIDTH:k * d + (c + 1) * MXU_WIDTH],
                        preferred_element_type=_F32) for k in range(3)]

    for c in range(d // MXU_WIDTH):
        cs = slice(c * MXU_WIDTH, (c + 1) * MXU_WIDTH)
        bg, cg, xv = in_proj(c)
        u = cg * xv
        y = bg * _causal_conv3(u, carry_ref[:, cs], cw_ref[:, cs])
        carry_ref[:, cs] = u[tm - SUBLANES:, :]
        y_ref[:, cs] = y.astype(_BF16)
    o_ref[...] = x + jnp.dot(y_ref[...], wout_ref[...], preferred_element_type=_F32)


def _mixer_a(x, g, w_in, conv_w, w_out, layer):
    s, d = x.shape
    tm = min(ROW_TILE, s)
    return pl.pallas_call(
        _mixer_a_kernel,
        grid=(s // tm,),
        in_specs=[
            pl.BlockSpec((tm, d), lambda i: (i, 0)),
            _resident((1, d)),
            _resident(w_in.shape, layer),
            _resident(conv_w.shape, layer),
            _resident(w_out.shape, layer),
        ],
        out_specs=pl.BlockSpec((tm, d), lambda i: (i, 0)),
        out_shape=jax.ShapeDtypeStruct((s, d), _F32),
        scratch_shapes=[pltpu.VMEM((SUBLANES, d), _F32), pltpu.VMEM((tm, d), _BF16)],
        compiler_params=_params(1),
        name="mixer_a",
    )(x, g, w_in, conv_w, w_out)


def _ffn_kernel(*refs, has_proj, has_final):
    refs = list(refs)
    x_ref = refs.pop(0)
    if has_proj:
        attn_ref = refs.pop(0)
        wo_ref = refs.pop(0)
    g_ref, wup_ref, cw_ref, cb_ref, wdn_ref = refs[:5]
    refs = refs[5:]
    if has_final:
        gf_ref = refs.pop(0)
    o_ref, carry_ref, act_ref = refs

    tm = x_ref.shape[0]
    f = wdn_ref.shape[0]

    @pl.when(pl.program_id(0) == 0)
    def _():
        carry_ref[...] = jnp.zeros_like(carry_ref)

    x = x_ref[...]
    if has_proj:
        x = x + jnp.dot(attn_ref[...], wo_ref[...], preferred_element_type=_F32)
    h = _rmsnorm(x, g_ref[...]).astype(_BF16)

    def up_proj(c):
        return [jnp.dot(h, wup_ref[:, k * f + c * MXU_WIDTH:k * f + (c + 1) * MXU_WIDTH],
                        preferred_element_type=_F32) for k in range(2)]

    for c in range(f // MXU_WIDTH):
        cs = slice(c * MXU_WIDTH, (c + 1) * MXU_WIDTH)
        gp, up = up_proj(c)
        gc = _causal_conv3(gp, carry_ref[:, cs], cw_ref[:, cs]) + cb_ref[:, cs]
        carry_ref[:, cs] = gp[tm - SUBLANES:, :]
        act_ref[:, cs] = (gc * jax.nn.sigmoid(gc) * up).astype(_BF16)
    acc = x + jnp.dot(act_ref[...], wdn_ref[...], preferred_element_type=_F32)
    if has_final:
        acc = _rmsnorm(acc, gf_ref[...])
    o_ref[...] = acc


def _ffn(x, g, w_up, conv_w, conv_b, w_down, layer, attn=None, w_o=None, o_layer=None,
         g_final=None):
    s, d = x.shape
    f = w_down.shape[1]
    tm = min(ROW_TILE, s)
    row_spec = pl.BlockSpec((tm, d), lambda i: (i, 0))
    args, specs = [x], [row_spec]
    if attn is not None:
        args += [attn, w_o]
        specs += [row_spec, _resident(w_o.shape, o_layer)]
    args += [g, w_up, conv_w, conv_b, w_down]
    specs += [_resident((1, d)), _resident(w_up.shape, layer), _resident(conv_w.shape, layer),
              _resident(conv_b.shape, layer), _resident(w_down.shape, layer)]
    if g_final is not None:
        args.append(g_final)
        specs.append(_resident((1, d)))
    return pl.pallas_call(
        functools.partial(_ffn_kernel, has_proj=attn is not None,
                          has_final=g_final is not None),
        grid=(s // tm,),
        in_specs=specs,
        out_specs=row_spec,
        out_shape=jax.ShapeDtypeStruct((s, d), _F32),
        scratch_shapes=[pltpu.VMEM((SUBLANES, f), _F32), pltpu.VMEM((tm, f), _BF16)],
        compiler_params=_params(1),
        name="conv_ffn",
    )(*args)


def _kv_kernel(x_ref, g_ref, wkv_ref, kt_ref, v_ref, km_ref):
    d = x_ref.shape[1]
    nb = kt_ref.shape[0]
    h = _rmsnorm(x_ref[...], g_ref[...]).astype(_BF16)
    k = jnp.dot(h, wkv_ref[:, :d], preferred_element_type=_F32)
    v = jnp.dot(h, wkv_ref[:, d:], preferred_element_type=_F32)
    v_ref[...] = v.astype(_BF16)
    means = []
    for b in range(nb):
        kb = k[b * BLOCK_SIZE:(b + 1) * BLOCK_SIZE, :]
        means.append(jnp.mean(kb, axis=0, keepdims=True))
        kt_ref[b] = kb.T.astype(_BF16)
    km_ref[...] = jnp.concatenate(means, axis=0)


def _shared_kv(x, g, w_kv):
    s, d = x.shape
    tm = min(KV_ROW_TILE, s)
    nb_step = tm // BLOCK_SIZE
    nb = s // BLOCK_SIZE
    return pl.pallas_call(
        _kv_kernel,
        grid=(s // tm,),
        in_specs=[
            pl.BlockSpec((tm, d), lambda i: (i, 0)),
            _resident((1, d)),
            _resident(w_kv.shape),
        ],
        out_specs=[
            pl.BlockSpec((nb_step, d, BLOCK_SIZE), lambda i: (i, 0, 0)),
            pl.BlockSpec((tm, d), lambda i: (i, 0)),
            pl.BlockSpec((nb_step, d), lambda i: (i, 0)),
        ],
        out_shape=[
            jax.ShapeDtypeStruct((nb, d, BLOCK_SIZE), _BF16),
            jax.ShapeDtypeStruct((s, d), _BF16),
            jax.ShapeDtypeStruct((nb, d), _F32),
        ],
        compiler_params=_params(1),
        name="shared_kv",
    )(x, g, w_kv)


def _q_gate_kernel(x_ref, g_ref, wq_ref, km_ref, o_ref):
    tm, d = x_ref.shape
    n_heads = d // HEAD_DIM
    scale = HEAD_DIM ** -0.5
    h = _rmsnorm(x_ref[...], g_ref[...]).astype(_BF16)
    q = jnp.dot(h, wq_ref[...], preferred_element_type=_F32)

    nbl = LANES // 2
    pos = pl.program_id(0) * tm + lax.broadcasted_iota(jnp.int32, (1, tm), 1)
    own = lax.shift_right_logical(pos, BLOCK_SIZE.bit_length() - 1)
    blk = lax.broadcasted_iota(jnp.int32, (nbl, tm), 0)
    valid = blk < own
    rel = (blk - own).astype(_F32)
    tail_row = lax.broadcasted_iota(jnp.int32, (LANES - nbl, tm), 0)

    for hd in range(n_heads):
        qh = q[:, hd * HEAD_DIM:(hd + 1) * HEAD_DIM]
        gate = lax.dot_general(km_ref[hd], qh, (((1,), (1,)), ((), ())),
                               preferred_element_type=_F32,
                               precision=lax.Precision.HIGHEST)
        gate = jnp.where(valid, gate, -jnp.inf)
        keep = blk == own
        for _ in range(TOP_K_BLOCKS):
            m = jnp.max(gate, axis=0, keepdims=True)
            first = jnp.min(jnp.where(gate == m, blk, nbl), axis=0, keepdims=True)
            pick = (blk == first) & (m > -jnp.inf)
            keep = keep | pick
            gate = jnp.where(pick, -jnp.inf, gate)
        slope = 2.0 ** (-8.0 * (hd + 1) / n_heads)
        bias = jnp.where(keep, (slope * BLOCK_SIZE) * rel, MASK_VALUE)
        tail = jnp.where(tail_row == 0, slope, 0.0)
        aux = jnp.concatenate([bias, tail], axis=0).T
        o_ref[:, hd * AUG:hd * AUG + HEAD_DIM] = (qh * scale).astype(_BF16)
        o_ref[:, hd * AUG + HEAD_DIM:(hd + 1) * AUG] = aux.astype(_BF16)


def _q_gate(x, g, w_q, km, layer):
    s, d = x.shape
    n_heads = d // HEAD_DIM
    tm = min(ROW_TILE, s)
    return pl.pallas_call(
        _q_gate_kernel,
        grid=(s // tm,),
        in_specs=[
            pl.BlockSpec((tm, d), lambda i: (i, 0)),
            _resident((1, d)),
            _resident(w_q.shape, layer),
            _resident(km.shape),
        ],
        out_specs=pl.BlockSpec((tm, n_heads * AUG), lambda i: (i, 0)),
        out_shape=jax.ShapeDtypeStruct((s, n_heads * AUG), _BF16),
        compiler_params=_params(1),
        name="q_gate",
    )(x, g, w_q, km)


def _attn_kernel(q_ref, kt_ref, v_ref, o_ref, kaux_ref, m_ref, alpha_ref, acc_ref, p_ref,
                 *, kb):
    tq = q_ref.shape[0]
    nb = kt_ref.shape[0]
    g = tq // BLOCK_SIZE
    i = pl.program_id(1)

    @pl.when((pl.program_id(0) == 0) & (i == 0))
    def _():
        row = lax.broadcasted_iota(jnp.int32, (HEAD_DIM, BLOCK_SIZE), 0)
        off = lax.broadcasted_iota(jnp.int32, (HEAD_DIM, BLOCK_SIZE), 1).astype(_F32)
        base = jnp.where(row == LANES // 2, off, 0.0)
        for n in range(nb):
            kaux_ref[n] = jnp.where(row == n, 1.0, base).astype(_BF16)

    lane = lax.broadcasted_iota(jnp.int32, (BLOCK_SIZE, HEAD_DIM), 1)
    ones_col = jnp.where(lane == 0, 1.0, 0.0).astype(_BF16)

    m_ref[...] = jnp.full_like(m_ref, MASK_VALUE)
    acc_ref[...] = jnp.zeros_like(acc_ref)

    def scores(r, blocks):
        q = q_ref[r * BLOCK_SIZE:(r + 1) * BLOCK_SIZE, :]
        return [jnp.dot(q, jnp.concatenate([kt_ref[n], kaux_ref[n]], axis=0),
                        preferred_element_type=_F32) for n in blocks]

    def softmax(r, parts, causal_last):
        rows = slice(r * BLOCK_SIZE, (r + 1) * BLOCK_SIZE)
        if causal_last:
            qpos = lax.broadcasted_iota(jnp.int32, (BLOCK_SIZE, BLOCK_SIZE), 0)
            kpos = lax.broadcasted_iota(jnp.int32, (BLOCK_SIZE, BLOCK_SIZE), 1)
            parts[-1] = jnp.where(kpos <= qpos, parts[-1], MASK_VALUE)
        s = parts[0] if len(parts) == 1 else jnp.concatenate(parts, axis=1)
        m_prev = m_ref[rows, :]
        m_new = jnp.maximum(m_prev, jnp.max(s, axis=-1, keepdims=True))
        alpha_ref[rows, :] = jnp.exp(m_prev - m_new)
        p_ref[rows, :s.shape[1]] = jnp.exp(s - m_new).astype(_BF16)
        m_ref[rows, :] = m_new

    def pv(r, blocks):
        rows = slice(r * BLOCK_SIZE, (r + 1) * BLOCK_SIZE)
        out = None
        for j, n in enumerate(blocks):
            v_aug = jnp.concatenate([v_ref[n], ones_col], axis=1)
            t = jnp.dot(p_ref[rows, j * BLOCK_SIZE:(j + 1) * BLOCK_SIZE], v_aug,
                        preferred_element_type=_F32)
            out = t if out is None else out + t
        acc_ref[rows, :] = alpha_ref[rows, :] * acc_ref[rows, :] + out

    lag = g // 2

    def stage(new_of, causal_last, old_of):
        for r in range(g):
            parts = scores(r, new_of(r))
            if r < lag:
                pv(r + g - lag, old_of(r + g - lag))
            softmax(r, parts, causal_last)
        for r in range(g - lag):
            pv(r, new_of(r))

    def past(t):
        return lambda r: [t * kb + j for j in range(kb)]

    diag = lambda r: [i * g + j for j in range(r + 1)]
    n_past = (i * g) // kb

    p_ref[...] = jnp.zeros_like(p_ref)
    alpha_ref[...] = jnp.ones_like(alpha_ref)

    def body(t, carry):
        stage(past(t), False, past(jnp.maximum(t - 1, 0)))
        return carry

    lax.fori_loop(0, n_past, body, 0)
    stage(diag, True, past(jnp.maximum(n_past - 1, 0)))
    for r in range(g - lag, g):
        pv(r, diag(r))

    acc = acc_ref[...]
    o_ref[...] = (acc[:, :HEAD_DIM] / acc[:, HEAD_DIM:HEAD_DIM + 1]).astype(o_ref.dtype)


def _attention(q_aug, kt, v):
    s = q_aug.shape[0]
    nb, d, _ = kt.shape
    n_heads = d // HEAD_DIM
    tq = min(Q_TILE, s)
    g = tq // BLOCK_SIZE
    kb = min(KV_BLOCKS_PER_STEP, g)
    assert g % kb == 0 and g >= 2
    v3 = v.reshape(nb, BLOCK_SIZE, d)
    return pl.pallas_call(
        functools.partial(_attn_kernel, kb=kb),
        grid=(n_heads, s // tq),
        in_specs=[
            pl.BlockSpec((tq, AUG), lambda h, i: (i, h)),
            pl.BlockSpec((nb, HEAD_DIM, BLOCK_SIZE), lambda h, i: (0, h, 0)),
            pl.BlockSpec((nb, BLOCK_SIZE, HEAD_DIM), lambda h, i: (0, 0, h)),
        ],
        out_specs=pl.BlockSpec((tq, HEAD_DIM), lambda h, i: (i, h)),
        out_shape=jax.ShapeDtypeStruct((s, d), _BF16),
        scratch_shapes=[
            pltpu.VMEM((nb, HEAD_DIM, BLOCK_SIZE), _BF16),
            pltpu.VMEM((tq, 1), _F32),
            pltpu.VMEM((tq, 1), _F32),
            pltpu.VMEM((tq, AUG), _F32),
            pltpu.VMEM((tq, max(kb, g) * BLOCK_SIZE), _BF16),
        ],
        compiler_params=_params(2),
        name="moba_attention",
    )(q_aug, kt, v3)


def kernel(x, norm_mix, norm_ffn, ffn_w_up, ffn_conv, ffn_conv_b, ffn_w_down,
           a_w_in, a_conv, a_w_out, kv_norm, w_kv, b_w_q, b_w_o, final_norm):
    b, s, d = x.shape
    assert b == 1 and s % BLOCK_SIZE == 0 and d % MXU_WIDTH == 0
    depth = norm_mix.shape[0]
    n_a = a_w_in.shape[0]
    n_heads = d // HEAD_DIM
    nb = s // BLOCK_SIZE
    assert nb <= LANES // 2, "block bias lanes hold at most LANES/2 key blocks"

    bf = lambda w: w.astype(_BF16)
    ffn_w_up, ffn_w_down = bf(ffn_w_up), bf(ffn_w_down)
    a_w_in, a_w_out = bf(a_w_in), bf(a_w_out)
    w_kv, b_w_q, b_w_o = bf(w_kv), bf(b_w_q), bf(b_w_o)

    conv_b = ffn_conv_b.reshape(depth, 1, -1)
    xs = x.reshape(s, d)
    kt = v = km = None
    for i in range(depth):
        gmix = norm_mix[i].reshape(1, d)
        attn = j = None
        if i < n_a:
            xs = _mixer_a(xs, gmix, a_w_in, a_conv, a_w_out, i)
        else:
            j = i - n_a
            attn = _attention(_q_gate(xs, gmix, b_w_q, km, j), kt, v)
        xs = _ffn(xs, norm_ffn[i].reshape(1, d), ffn_w_up, ffn_conv, conv_b, ffn_w_down, i,
                  attn=attn, w_o=b_w_o if attn is not None else None, o_layer=j,
                  g_final=final_norm.reshape(1, d) if i == depth - 1 else None)
        if i == n_a - 1:
            kt, v, km = _shared_kv(xs, kv_norm.reshape(1, d), w_kv)
            km = km.reshape(nb, n_heads, HEAD_DIM).transpose(1, 0, 2)
            km = jnp.pad(km, ((0, 0), (0, LANES // 2 - nb), (0, 0)))
    return xs.reshape(b, s, d)
```

```python
import functools

import jax
import jax.numpy as jnp
from jax import lax
from jax.experimental import pallas as pl
from jax.experimental.pallas import tpu as pltpu

HEAD_DIM = 128
BLOCK_SIZE = 256
TOP_K_BLOCKS = 3
CONV_WIDTH = 3
EPS = 1e-6
MASK_VALUE = -1e30

LANES = 128
SUBLANES = 8
MXU_WIDTH = 256
VMEM_LIMIT_BYTES = 56 * 1024 * 1024

ROW_TILE = 512
KV_ROW_TILE = 2048
Q_TILE = 2048
KV_BLOCKS_PER_STEP = 4
PIPELINE_LAG = 6
AUG = 2 * HEAD_DIM

_BF16 = jnp.bfloat16
_F32 = jnp.float32


def _resident(shape, layer=None):
    nd = len(shape)
    if layer is None:
        return pl.BlockSpec(shape, lambda *_: (0,) * nd, pipeline_mode=pl.Buffered(1))
    return pl.BlockSpec((None,) + tuple(shape[1:]), lambda *_: (layer,) + (0,) * (nd - 1),
                        pipeline_mode=pl.Buffered(1))


def _params(n_axes):
    return pltpu.CompilerParams(
        dimension_semantics=("arbitrary",) * n_axes,
        vmem_limit_bytes=VMEM_LIMIT_BYTES)


def _rmsnorm(x, g):
    ms = jnp.mean(x * x, axis=-1, keepdims=True)
    return x * lax.rsqrt(ms + EPS) * g


def _causal_conv3(v, prev, w):
    rows = lax.broadcasted_iota(jnp.int32, v.shape, 0)
    p1 = prev[SUBLANES - 1:SUBLANES, :]
    p2 = prev[SUBLANES - 2:SUBLANES - 1, :]
    v1 = jnp.where(rows == 0, p1, pltpu.roll(v, 1, 0))
    v2 = jnp.where(rows == 0, p2, jnp.where(rows == 1, p1, pltpu.roll(v, 2, 0)))
    return w[0:1, :] * v2 + w[1:2, :] * v1 + w[2:3, :] * v


def _mixer_a_kernel(x_ref, g_ref, win_ref, cw_ref, wout_ref, o_ref, carry_ref, y_ref):
    d = x_ref.shape[1]
    tm = x_ref.shape[0]

    @pl.when(pl.program_id(0) == 0)
    def _():
        carry_ref[...] = jnp.zeros_like(carry_ref)

    x = x_ref[...]
    h = _rmsnorm(x, g_ref[...]).astype(_BF16)

    def in_proj(c):
        return [jnp.dot(h, win_ref[:, k * d + c * MXU_WIDTH:k * d + (c + 1) * MXU_WIDTH],
                        preferred_element_type=_F32) for k in range(3)]

    for c in range(d // MXU_WIDTH):
        cs = slice(c * MXU_WIDTH, (c + 1) * MXU_WIDTH)
        bg, cg, xv = in_proj(c)
        u = cg * xv
        y = bg * _causal_conv3(u, carry_ref[:, cs], cw_ref[:, cs])
        carry_ref[:, cs] = u[tm - SUBLANES:, :]
        y_ref[:, cs] = y.astype(_BF16)
    o_ref[...] = x + jnp.dot(y_ref[...], wout_ref[...], preferred_element_type=_F32)


def _mixer_a(x, g, w_in, conv_w, w_out, layer):
    s, d = x.shape
    tm = min(ROW_TILE, s)
    return pl.pallas_call(
        _mixer_a_kernel,
        grid=(s // tm,),
        in_specs=[
            pl.BlockSpec((tm, d), lambda i: (i, 0)),
            _resident((1, d)),
            _resident(w_in.shape, layer),
            _resident(conv_w.shape, layer),
            _resident(w_out.shape, layer),
        ],
        out_specs=pl.BlockSpec((tm, d), lambda i: (i, 0)),
        out_shape=jax.ShapeDtypeStruct((s, d), _F32),
        scratch_shapes=[pltpu.VMEM((SUBLANES, d), _F32), pltpu.VMEM((tm, d), _BF16)],
        compiler_params=_params(1),
        name="mixer_a",
    )(x, g, w_in, conv_w, w_out)


def _ffn_kernel(*refs, has_proj, has_final):
    refs = list(refs)
    x_ref = refs.pop(0)
    if has_proj:
        attn_ref = refs.pop(0)
        wo_ref = refs.pop(0)
    g_ref, wup_ref, cw_ref, cb_ref, wdn_ref = refs[:5]
    refs = refs[5:]
    if has_final:
        gf_ref = refs.pop(0)
    o_ref, carry_ref, act_ref = refs

    tm = x_ref.shape[0]
    f = wdn_ref.shape[0]

    @pl.when(pl.program_id(0) == 0)
    def _():
        carry_ref[...] = jnp.zeros_like(carry_ref)

    x = x_ref[...]
    if has_proj:
        x = x + jnp.dot(attn_ref[...], wo_ref[...], preferred_element_type=_F32)
    h = _rmsnorm(x, g_ref[...]).astype(_BF16)

    def up_proj(c):
        return [jnp.dot(h, wup_ref[:, k * f + c * MXU_WIDTH:k * f + (c + 1) * MXU_WIDTH],
                        preferred_element_type=_F32) for k in range(2)]

    for c in range(f // MXU_WIDTH):
        cs = slice(c * MXU_WIDTH, (c + 1) * MXU_WIDTH)
        gp, up = up_proj(c)
        gc = _causal_conv3(gp, carry_ref[:, cs], cw_ref[:, cs]) + cb_ref[:, cs]
        carry_ref[:, cs] = gp[tm - SUBLANES:, :]
        act_ref[:, cs] = (gc * jax.nn.sigmoid(gc) * up).astype(_BF16)
    acc = x + jnp.dot(act_ref[...], wdn_ref[...], preferred_element_type=_F32)
    if has_final:
        acc = _rmsnorm(acc, gf_ref[...])
    o_ref[...] = acc


def _ffn(x, g, w_up, conv_w, conv_b, w_down, layer, attn=None, w_o=None, o_layer=None,
         g_final=None):
    s, d = x.shape
    f = w_down.shape[1]
    tm = min(ROW_TILE, s)
    row_spec = pl.BlockSpec((tm, d), lambda i: (i, 0))
    args, specs = [x], [row_spec]
    if attn is not None:
        args += [attn, w_o]
        specs += [row_spec, _resident(w_o.shape, o_layer)]
    args += [g, w_up, conv_w, conv_b, w_down]
    specs += [_resident((1, d)), _resident(w_up.shape, layer), _resident(conv_w.shape, layer),
              _resident(conv_b.shape, layer), _resident(w_down.shape, layer)]
    if g_final is not None:
        args.append(g_final)
        specs.append(_resident((1, d)))
    return pl.pallas_call(
        functools.partial(_ffn_kernel, has_proj=attn is not None,
                          has_final=g_final is not None),
        grid=(s // tm,),
        in_specs=specs,
        out_specs=row_spec,
        out_shape=jax.ShapeDtypeStruct((s, d), _F32),
        scratch_shapes=[pltpu.VMEM((SUBLANES, f), _F32), pltpu.VMEM((tm, f), _BF16)],
        compiler_params=_params(1),
        name="conv_ffn",
    )(*args)


def _kv_kernel(x_ref, g_ref, wkv_ref, kt_ref, v_ref, km_ref):
    d = x_ref.shape[1]
    nb = kt_ref.shape[0]
    h = _rmsnorm(x_ref[...], g_ref[...]).astype(_BF16)
    k = jnp.dot(h, wkv_ref[:, :d], preferred_element_type=_F32)
    v = jnp.dot(h, wkv_ref[:, d:], preferred_element_type=_F32)
    v_ref[...] = v.astype(_BF16)
    means = []
    for b in range(nb):
        kb = k[b * BLOCK_SIZE:(b + 1) * BLOCK_SIZE, :]
        means.append(jnp.mean(kb, axis=0, keepdims=True))
        kt_ref[b] = kb.T.astype(_BF16)
    km_ref[...] = jnp.concatenate(means, axis=0)


def _shared_kv(x, g, w_kv):
    s, d = x.shape
    tm = min(KV_ROW_TILE, s)
    nb_step = tm // BLOCK_SIZE
    nb = s // BLOCK_SIZE
    return pl.pallas_call(
        _kv_kernel,
        grid=(s // tm,),
        in_specs=[
            pl.BlockSpec((tm, d), lambda i: (i, 0)),
            _resident((1, d)),
            _resident(w_kv.shape),
        ],
        out_specs=[
            pl.BlockSpec((nb_step, d, BLOCK_SIZE), lambda i: (i, 0, 0)),
            pl.BlockSpec((tm, d), lambda i: (i, 0)),
            pl.BlockSpec((nb_step, d), lambda i: (i, 0)),
        ],
        out_shape=[
            jax.ShapeDtypeStruct((nb, d, BLOCK_SIZE), _BF16),
            jax.ShapeDtypeStruct((s, d), _BF16),
            jax.ShapeDtypeStruct((nb, d), _F32),
        ],
        compiler_params=_params(1),
        name="shared_kv",
    )(x, g, w_kv)


def _q_gate_kernel(x_ref, g_ref, wq_ref, km_ref, o_ref):
    tm, d = x_ref.shape
    n_heads = d // HEAD_DIM
    scale = HEAD_DIM ** -0.5
    h = _rmsnorm(x_ref[...], g_ref[...]).astype(_BF16)
    q = jnp.dot(h, wq_ref[...], preferred_element_type=_F32)

    nbl = LANES // 2
    pos = pl.program_id(0) * tm + lax.broadcasted_iota(jnp.int32, (1, tm), 1)
    own = lax.shift_right_logical(pos, BLOCK_SIZE.bit_length() - 1)
    blk = lax.broadcasted_iota(jnp.int32, (nbl, tm), 0)
    valid = blk < own
    rel = (blk - own).astype(_F32)
    tail_row = lax.broadcasted_iota(jnp.int32, (LANES - nbl, tm), 0)

    for hd in range(n_heads):
        qh = q[:, hd * HEAD_DIM:(hd + 1) * HEAD_DIM]
        gate = lax.dot_general(km_ref[hd], qh, (((1,), (1,)), ((), ())),
                               preferred_element_type=_F32,
                               precision=lax.Precision.HIGHEST)
        gate = jnp.where(valid, gate, -jnp.inf)
        keep = blk == own
        for _ in range(TOP_K_BLOCKS):
            m = jnp.max(gate, axis=0, keepdims=True)
            first = jnp.min(jnp.where(gate == m, blk, nbl), axis=0, keepdims=True)
            pick = (blk == first) & (m > -jnp.inf)
            keep = keep | pick
            gate = jnp.where(pick, -jnp.inf, gate)
        slope = 2.0 ** (-8.0 * (hd + 1) / n_heads)
        bias = jnp.where(keep, (slope * BLOCK_SIZE) * rel, MASK_VALUE)
        tail = jnp.where(tail_row == 0, slope, 0.0)
        aux = jnp.concatenate([bias, tail], axis=0).T
        o_ref[:, hd * AUG:hd * AUG + HEAD_DIM] = (qh * scale).astype(_BF16)
        o_ref[:, hd * AUG + HEAD_DIM:(hd + 1) * AUG] = aux.astype(_BF16)


def _q_gate(x, g, w_q, km, layer):
    s, d = x.shape
    n_heads = d // HEAD_DIM
    tm = min(ROW_TILE, s)
    return pl.pallas_call(
        _q_gate_kernel,
        grid=(s // tm,),
        in_specs=[
            pl.BlockSpec((tm, d), lambda i: (i, 0)),
            _resident((1, d)),
            _resident(w_q.shape, layer),
            _resident(km.shape),
        ],
        out_specs=pl.BlockSpec((tm, n_heads * AUG), lambda i: (i, 0)),
        out_shape=jax.ShapeDtypeStruct((s, n_heads * AUG), _BF16),
        compiler_params=_params(1),
        name="q_gate",
    )(x, g, w_q, km)


def _attn_kernel(q_ref, kt_ref, v_ref, o_ref, kaux_ref, m_ref, alpha_ref, acc_ref, p_ref,
                 *, kb):
    tq = q_ref.shape[0]
    nb = kt_ref.shape[0]
    g = tq // BLOCK_SIZE
    i = pl.program_id(1)

    @pl.when((pl.program_id(0) == 0) & (i == 0))
    def _():
        row = lax.broadcasted_iota(jnp.int32, (HEAD_DIM, BLOCK_SIZE), 0)
        off = lax.broadcasted_iota(jnp.int32, (HEAD_DIM, BLOCK_SIZE), 1).astype(_F32)
        base = jnp.where(row == LANES // 2, off, 0.0)
        for n in range(nb):
            kaux_ref[n] = jnp.where(row == n, 1.0, base).astype(_BF16)

    lane = lax.broadcasted_iota(jnp.int32, (BLOCK_SIZE, HEAD_DIM), 1)
    ones_col = jnp.where(lane == 0, 1.0, 0.0).astype(_BF16)

    m_ref[...] = jnp.full_like(m_ref, MASK_VALUE)
    acc_ref[...] = jnp.zeros_like(acc_ref)

    def scores(r, blocks):
        q = q_ref[r * BLOCK_SIZE:(r + 1) * BLOCK_SIZE, :]
        return [jnp.dot(q, jnp.concatenate([kt_ref[n], kaux_ref[n]], axis=0),
                        preferred_element_type=_F32) for n in blocks]

    def softmax(r, parts, causal_last):
        rows = slice(r * BLOCK_SIZE, (r + 1) * BLOCK_SIZE)
        if causal_last:
            qpos = lax.broadcasted_iota(jnp.int32, (BLOCK_SIZE, BLOCK_SIZE), 0)
            kpos = lax.broadcasted_iota(jnp.int32, (BLOCK_SIZE, BLOCK_SIZE), 1)
            parts[-1] = jnp.where(kpos <= qpos, parts[-1], MASK_VALUE)
        s = parts[0] if len(parts) == 1 else jnp.concatenate(parts, axis=1)
        m_prev = m_ref[rows, :]
        m_new = jnp.maximum(m_prev, jnp.max(s, axis=-1, keepdims=True))
        alpha_ref[rows, :] = jnp.exp(m_prev - m_new)
        p_ref[rows, :s.shape[1]] = jnp.exp(s - m_new).astype(_BF16)
        m_ref[rows, :] = m_new

    def pv(r, blocks):
        rows = slice(r * BLOCK_SIZE, (r + 1) * BLOCK_SIZE)
        out = None
        for j, n in enumerate(blocks):
            v_aug = jnp.concatenate([v_ref[n], ones_col], axis=1)
            t = jnp.dot(p_ref[rows, j * BLOCK_SIZE:(j + 1) * BLOCK_SIZE], v_aug,
                        preferred_element_type=_F32)
            out = t if out is None else out + t
        acc_ref[rows, :] = alpha_ref[rows, :] * acc_ref[rows, :] + out

    lag = min(PIPELINE_LAG, g - 1)

    def stage(new_of, causal_last, old_of, descending=False):
        order = list(range(g))[::-1] if descending else list(range(g))
        for k, r in enumerate(order):
            parts = scores(r, new_of(r))
            if k < lag and old_of is not None:
                u = r if descending else g - lag + k
                pv(u, old_of(u))
            softmax(r, parts, causal_last)
        for r in order[:g - lag]:
            pv(r, new_of(r))
        return order[g - lag:]

    def past(t):
        return lambda r: [t * kb + j for j in range(kb)]

    diag = lambda r: [i * g + j for j in range(r + 1)]
    n_past = (i * g) // kb

    @pl.when(n_past == 0)
    def _():
        stage(diag, True, None, descending=True)

    @pl.when(n_past > 0)
    def _():
        stage(past(0), False, None)

        def body(t, carry):
            stage(past(t), False, past(t - 1))
            return carry

        lax.fori_loop(1, n_past, body, 0)
        stage(diag, True, past(n_past - 1), descending=True)

    for r in list(range(g))[::-1][g - lag:]:
        pv(r, diag(r))

    acc = acc_ref[...]
    o_ref[...] = (acc[:, :HEAD_DIM] / acc[:, HEAD_DIM:HEAD_DIM + 1]).astype(o_ref.dtype)


def _attention(q_aug, kt, v):
    s = q_aug.shape[0]
    nb, d, _ = kt.shape
    n_heads = d // HEAD_DIM
    tq = min(Q_TILE, s)
    g = tq // BLOCK_SIZE
    kb = min(KV_BLOCKS_PER_STEP, g)
    assert g % kb == 0 and g >= 2
    v3 = v.reshape(nb, BLOCK_SIZE, d)
    return pl.pallas_call(
        functools.partial(_attn_kernel, kb=kb),
        grid=(n_heads, s // tq),
        in_specs=[
            pl.BlockSpec((tq, AUG), lambda h, i: (i, h)),
            pl.BlockSpec((nb, HEAD_DIM, BLOCK_SIZE), lambda h, i: (0, h, 0)),
            pl.BlockSpec((nb, BLOCK_SIZE, HEAD_DIM), lambda h, i: (0, 0, h)),
        ],
        out_specs=pl.BlockSpec((tq, HEAD_DIM), lambda h, i: (i, h)),
        out_shape=jax.ShapeDtypeStruct((s, d), _BF16),
        scratch_shapes=[
            pltpu.VMEM((nb, HEAD_DIM, BLOCK_SIZE), _BF16),
            pltpu.VMEM((tq, 1), _F32),
            pltpu.VMEM((tq, 1), _F32),
            pltpu.VMEM((tq, AUG), _F32),
            pltpu.VMEM((tq, max(kb, g) * BLOCK_SIZE), _BF16),
        ],
        compiler_params=_params(2),
        name="moba_attention",
    )(q_aug, kt, v3)


def kernel(x, norm_mix, norm_ffn, ffn_w_up, ffn_conv, ffn_conv_b, ffn_w_down,
           a_w_in, a_conv, a_w_out, kv_norm, w_kv, b_w_q, b_w_o, final_norm):
    b, s, d = x.shape
    assert b == 1 and s % BLOCK_SIZE == 0 and d % MXU_WIDTH == 0
    depth = norm_mix.shape[0]
    n_a = a_w_in.shape[0]
    n_heads = d // HEAD_DIM
    nb = s // BLOCK_SIZE
    assert nb <= LANES // 2, "block bias lanes hold at most LANES/2 key blocks"

    bf = lambda w: w.astype(_BF16)
    ffn_w_up, ffn_w_down = bf(ffn_w_up), bf(ffn_w_down)
    a_w_in, a_w_out = bf(a_w_in), bf(a_w_out)
    w_kv, b_w_q, b_w_o = bf(w_kv), bf(b_w_q), bf(b_w_o)

    conv_b = ffn_conv_b.reshape(depth, 1, -1)
    xs = x.reshape(s, d)
    kt = v = km = None
    for i in range(depth):
        gmix = norm_mix[i].reshape(1, d)
        attn = j = None
        if i < n_a:
            xs = _mixer_a(xs, gmix, a_w_in, a_conv, a_w_out, i)
        else:
            j = i - n_a
            attn = _attention(_q_gate(xs, gmix, b_w_q, km, j), kt, v)
        xs = _ffn(xs, norm_ffn[i].reshape(1, d), ffn_w_up, ffn_conv, conv_b, ffn_w_down, i,
                  attn=attn, w_o=b_w_o if attn is not None else None, o_layer=j,
                  g_final=final_norm.reshape(1, d) if i == depth - 1 else None)
        if i == n_a - 1:
            kt, v, km = _shared_kv(xs, kv_norm.reshape(1, d), w_kv)
            km = km.reshape(nb, n_heads, HEAD_DIM).transpose(1, 0, 2)
            km = jnp.pad(km, ((0, 0), (0, LANES // 2 - nb), (0, 0)))
    return xs.reshape(b, s, d)
```

```python
import functools

import numpy as np
import jax
import jax.numpy as jnp
from jax import lax
from jax.experimental import pallas as pl
from jax.experimental.pallas import tpu as pltpu

HEAD_DIM = 128
BLOCK_SIZE = 256
TOP_K_BLOCKS = 3
CONV_WIDTH = 3
EPS = 1e-6
MASK_VALUE = -1e30

LANES = 128
SUBLANES = 8
MXU_WIDTH = 256
VMEM_LIMIT_BYTES = 56 * 1024 * 1024

ROW_TILE = 512
KV_ROW_TILE = 2048
Q_TILE = 2048
KV_BLOCKS_PER_STEP = 4
PIPELINE_LAG = 6
AUG = 2 * HEAD_DIM
LOG2E = 1.4426950408889634
SPLIT = 3

_BF16 = jnp.bfloat16
_F32 = jnp.float32


def _resident(shape, layer=None):
    nd = len(shape)
    if layer is None:
        return pl.BlockSpec(shape, lambda *_: (0,) * nd, pipeline_mode=pl.Buffered(1))
    return pl.BlockSpec((None,) + tuple(shape[1:]), lambda *_: (layer,) + (0,) * (nd - 1),
                        pipeline_mode=pl.Buffered(1))


def _params(n_axes):
    return pltpu.CompilerParams(
        dimension_semantics=("arbitrary",) * n_axes,
        vmem_limit_bytes=VMEM_LIMIT_BYTES)


def _rmsnorm(x, g):
    ms = jnp.mean(x * x, axis=-1, keepdims=True)
    return x * lax.rsqrt(ms + EPS) * g


def _causal_conv3(v, prev, w):
    rows = lax.broadcasted_iota(jnp.int32, v.shape, 0)
    p1 = prev[SUBLANES - 1:SUBLANES, :]
    p2 = prev[SUBLANES - 2:SUBLANES - 1, :]
    v1 = jnp.where(rows == 0, p1, pltpu.roll(v, 1, 0))
    v2 = jnp.where(rows == 0, p2, jnp.where(rows == 1, p1, pltpu.roll(v, 2, 0)))
    return w[0:1, :] * v2 + w[1:2, :] * v1 + w[2:3, :] * v


def _mixer_a_kernel(x_ref, g_ref, win_ref, cw_ref, wout_ref, o_ref, carry_ref, y_ref):
    d = x_ref.shape[1]
    tm = x_ref.shape[0]

    @pl.when(pl.program_id(0) == 0)
    def _():
        carry_ref[...] = jnp.zeros_like(carry_ref)

    x = x_ref[...]
    h = _rmsnorm(x, g_ref[...]).astype(_BF16)

    def in_proj(c):
        return [jnp.dot(h, win_ref[:, k * d + c * MXU_WIDTH:k * d + (c + 1) * MXU_WIDTH],
                        preferred_element_type=_F32) for k in range(3)]

    for c in range(d // MXU_WIDTH):
        cs = slice(c * MXU_WIDTH, (c + 1) * MXU_WIDTH)
        bg, cg, xv = in_proj(c)
        u = cg * xv
        y = bg * _causal_conv3(u, carry_ref[:, cs], cw_ref[:, cs])
        carry_ref[:, cs] = u[tm - SUBLANES:, :]
        y_ref[:, cs] = y.astype(_BF16)
    o_ref[...] = x + jnp.dot(y_ref[...], wout_ref[...], preferred_element_type=_F32)


def _mixer_a(x, g, w_in, conv_w, w_out, layer):
    s, d = x.shape
    tm = min(ROW_TILE, s)
    return pl.pallas_call(
        _mixer_a_kernel,
        grid=(s // tm,),
        in_specs=[
            pl.BlockSpec((tm, d), lambda i: (i, 0)),
            _resident((1, d)),
            _resident(w_in.shape, layer),
            _resident(conv_w.shape, layer),
            _resident(w_out.shape, layer),
        ],
        out_specs=pl.BlockSpec((tm, d), lambda i: (i, 0)),
        out_shape=jax.ShapeDtypeStruct((s, d), _F32),
        scratch_shapes=[pltpu.VMEM((SUBLANES, d), _F32), pltpu.VMEM((tm, d), _BF16)],
        compiler_params=_params(1),
        name="mixer_a",
    )(x, g, w_in, conv_w, w_out)


def _ffn_kernel(*refs, has_proj, has_final):
    refs = list(refs)
    x_ref = refs.pop(0)
    if has_proj:
        attn_ref = refs.pop(0)
        wo_ref = refs.pop(0)
    g_ref, wup_ref, cw_ref, cb_ref, wdn_ref = refs[:5]
    refs = refs[5:]
    if has_final:
        gf_ref = refs.pop(0)
    o_ref, carry_ref, act_ref = refs

    tm = x_ref.shape[0]
    f = wdn_ref.shape[0]

    @pl.when(pl.program_id(0) == 0)
    def _():
        carry_ref[...] = jnp.zeros_like(carry_ref)

    x = x_ref[...]
    if has_proj:
        x = x + jnp.dot(attn_ref[...], wo_ref[...], preferred_element_type=_F32)
    h = _rmsnorm(x, g_ref[...]).astype(_BF16)

    def up_proj(c):
        return [jnp.dot(h, wup_ref[:, k * f + c * MXU_WIDTH:k * f + (c + 1) * MXU_WIDTH],
                        preferred_element_type=_F32) for k in range(2)]

    for c in range(f // MXU_WIDTH):
        cs = slice(c * MXU_WIDTH, (c + 1) * MXU_WIDTH)
        gp, up = up_proj(c)
        gc = _causal_conv3(gp, carry_ref[:, cs], cw_ref[:, cs]) + cb_ref[:, cs]
        carry_ref[:, cs] = gp[tm - SUBLANES:, :]
        act_ref[:, cs] = (gc * jax.nn.sigmoid(gc) * up).astype(_BF16)
    acc = x + jnp.dot(act_ref[...], wdn_ref[...], preferred_element_type=_F32)
    if has_final:
        acc = _rmsnorm(acc, gf_ref[...])
    o_ref[...] = acc


def _ffn(x, g, w_up, conv_w, conv_b, w_down, layer, attn=None, w_o=None, o_layer=None,
         g_final=None):
    s, d = x.shape
    f = w_down.shape[1]
    tm = min(ROW_TILE, s)
    row_spec = pl.BlockSpec((tm, d), lambda i: (i, 0))
    args, specs = [x], [row_spec]
    if attn is not None:
        args += [attn, w_o]
        specs += [row_spec, _resident(w_o.shape, o_layer)]
    args += [g, w_up, conv_w, conv_b, w_down]
    specs += [_resident((1, d)), _resident(w_up.shape, layer), _resident(conv_w.shape, layer),
              _resident(conv_b.shape, layer), _resident(w_down.shape, layer)]
    if g_final is not None:
        args.append(g_final)
        specs.append(_resident((1, d)))
    return pl.pallas_call(
        functools.partial(_ffn_kernel, has_proj=attn is not None,
                          has_final=g_final is not None),
        grid=(s // tm,),
        in_specs=specs,
        out_specs=row_spec,
        out_shape=jax.ShapeDtypeStruct((s, d), _F32),
        scratch_shapes=[pltpu.VMEM((SUBLANES, f), _F32), pltpu.VMEM((tm, f), _BF16)],
        compiler_params=_params(1),
        name="conv_ffn",
    )(*args)


def _kv_kernel(x_ref, g_ref, wkv_ref, kt_ref, v_ref, km_ref):
    d = x_ref.shape[1]
    nb = kt_ref.shape[0]
    h = _rmsnorm(x_ref[...], g_ref[...]).astype(_BF16)
    k = jnp.dot(h, wkv_ref[:, :d], preferred_element_type=_F32)
    v = jnp.dot(h, wkv_ref[:, d:], preferred_element_type=_F32)
    v_ref[...] = v.astype(_BF16)
    means = []
    for b in range(nb):
        kb = k[b * BLOCK_SIZE:(b + 1) * BLOCK_SIZE, :]
        means.append(jnp.mean(kb, axis=0, keepdims=True))
        kt_ref[b] = kb.T.astype(_BF16)
    km_ref[...] = jnp.concatenate(means, axis=0)


def _shared_kv(x, g, w_kv):
    s, d = x.shape
    tm = min(KV_ROW_TILE, s)
    nb_step = tm // BLOCK_SIZE
    nb = s // BLOCK_SIZE
    return pl.pallas_call(
        _kv_kernel,
        grid=(s // tm,),
        in_specs=[
            pl.BlockSpec((tm, d), lambda i: (i, 0)),
            _resident((1, d)),
            _resident(w_kv.shape),
        ],
        out_specs=[
            pl.BlockSpec((nb_step, d, BLOCK_SIZE), lambda i: (i, 0, 0)),
            pl.BlockSpec((tm, d), lambda i: (i, 0)),
            pl.BlockSpec((nb_step, d), lambda i: (i, 0)),
        ],
        out_shape=[
            jax.ShapeDtypeStruct((nb, d, BLOCK_SIZE), _BF16),
            jax.ShapeDtypeStruct((s, d), _BF16),
            jax.ShapeDtypeStruct((nb, d), _F32),
        ],
        compiler_params=_params(1),
        name="shared_kv",
    )(x, g, w_kv)


def _bf16_pieces(x):
    out = []
    for _ in range(SPLIT):
        piece = float(np.float32(x).astype(_BF16))
        out.append(piece)
        x -= piece
    return out


def _q_gate_kernel(x_ref, g_ref, wq_ref, km_ref, o_ref):
    tm, d = x_ref.shape
    n_heads = d // HEAD_DIM
    scale = HEAD_DIM ** -0.5
    h = _rmsnorm(x_ref[...], g_ref[...]).astype(_BF16)
    q = jnp.dot(h, wq_ref[...], preferred_element_type=_F32)

    nbl = LANES // 2
    pos = pl.program_id(0) * tm + lax.broadcasted_iota(jnp.int32, (1, tm), 1)
    own = lax.shift_right_logical(pos, BLOCK_SIZE.bit_length() - 1)
    blk = lax.broadcasted_iota(jnp.int32, (nbl, tm), 0)
    valid = blk < own
    tail_row = lax.broadcasted_iota(jnp.int32, (LANES - nbl, tm), 0)

    for hd in range(n_heads):
        qh = q[:, hd * HEAD_DIM:(hd + 1) * HEAD_DIM]
        gate = lax.dot_general(km_ref[hd], qh, (((1,), (1,)), ((), ())),
                               preferred_element_type=_F32,
                               precision=lax.Precision.HIGHEST)
        gate = jnp.where(valid, gate, -jnp.inf)
        keep = blk == own
        for _ in range(TOP_K_BLOCKS):
            m = jnp.max(gate, axis=0, keepdims=True)
            first = jnp.min(jnp.where(gate == m, blk, nbl), axis=0, keepdims=True)
            pick = (blk == first) & (m > -jnp.inf)
            keep = keep | pick
            gate = jnp.where(pick, -jnp.inf, gate)
        bias = jnp.where(keep, 0.0, MASK_VALUE)
        slope = 2.0 ** (-8.0 * (hd + 1) / n_heads)
        tail = jnp.zeros((LANES - nbl, tm), _F32)
        for k, piece in enumerate(_bf16_pieces(slope * LOG2E)):
            tail = jnp.where(tail_row == k, piece * BLOCK_SIZE, tail)
            tail = jnp.where(tail_row == SPLIT + k, piece, tail)
        aux = jnp.concatenate([bias, tail], axis=0).T
        o_ref[:, hd * AUG:hd * AUG + HEAD_DIM] = (qh * (scale * LOG2E)).astype(_BF16)
        o_ref[:, hd * AUG + HEAD_DIM:(hd + 1) * AUG] = aux.astype(_BF16)


def _q_gate(x, g, w_q, km, layer):
    s, d = x.shape
    n_heads = d // HEAD_DIM
    tm = min(ROW_TILE, s)
    return pl.pallas_call(
        _q_gate_kernel,
        grid=(s // tm,),
        in_specs=[
            pl.BlockSpec((tm, d), lambda i: (i, 0)),
            _resident((1, d)),
            _resident(w_q.shape, layer),
            _resident(km.shape),
        ],
        out_specs=pl.BlockSpec((tm, n_heads * AUG), lambda i: (i, 0)),
        out_shape=jax.ShapeDtypeStruct((s, n_heads * AUG), _BF16),
        compiler_params=_params(1),
        name="q_gate",
    )(x, g, w_q, km)


def _attn_kernel(q_ref, kt_ref, v_ref, o_ref, kaux_ref, m_ref, alpha_ref, acc_ref, p_ref,
                 *, kb):
    tq = q_ref.shape[0]
    nb = kt_ref.shape[0]
    g = tq // BLOCK_SIZE
    i = pl.program_id(1)

    @pl.when((pl.program_id(0) == 0) & (i == 0))
    def _():
        nbl = LANES // 2
        row = lax.broadcasted_iota(jnp.int32, (HEAD_DIM, BLOCK_SIZE), 0)
        off = lax.broadcasted_iota(jnp.int32, (HEAD_DIM, BLOCK_SIZE), 1).astype(_F32)
        is_off = (row >= nbl + SPLIT) & (row < nbl + 2 * SPLIT)
        is_id = (row >= nbl) & (row < nbl + SPLIT)
        base = jnp.where(is_off, off, 0.0)
        for n in range(nb):
            kaux_ref[n] = jnp.where(row == n, 1.0,
                                    jnp.where(is_id, float(n), base)).astype(_BF16)

    lane = lax.broadcasted_iota(jnp.int32, (BLOCK_SIZE, HEAD_DIM), 1)
    ones_col = jnp.where(lane == 0, 1.0, 0.0).astype(_BF16)

    m_ref[...] = jnp.full_like(m_ref, MASK_VALUE)
    acc_ref[...] = jnp.zeros_like(acc_ref)

    def scores(r, blocks):
        q = q_ref[r * BLOCK_SIZE:(r + 1) * BLOCK_SIZE, :]
        return [jnp.dot(q, jnp.concatenate([kt_ref[n], kaux_ref[n]], axis=0),
                        preferred_element_type=_F32) for n in blocks]

    def softmax(r, parts, causal_last):
        rows = slice(r * BLOCK_SIZE, (r + 1) * BLOCK_SIZE)
        if causal_last:
            qpos = lax.broadcasted_iota(jnp.int32, (BLOCK_SIZE, BLOCK_SIZE), 0)
            kpos = lax.broadcasted_iota(jnp.int32, (BLOCK_SIZE, BLOCK_SIZE), 1)
            parts[-1] = jnp.where(kpos <= qpos, parts[-1], MASK_VALUE)
        s = parts[0] if len(parts) == 1 else jnp.concatenate(parts, axis=1)
        m_prev = m_ref[rows, :]
        m_new = jnp.maximum(m_prev, jnp.max(s, axis=-1, keepdims=True))
        alpha_ref[rows, :] = jnp.exp2(m_prev - m_new)
        p_ref[rows, :s.shape[1]] = jnp.exp2(s - m_new).astype(_BF16)
        m_ref[rows, :] = m_new

    def pv(r, blocks):
        rows = slice(r * BLOCK_SIZE, (r + 1) * BLOCK_SIZE)
        out = None
        for j, n in enumerate(blocks):
            v_aug = jnp.concatenate([v_ref[n], ones_col], axis=1)
            t = jnp.dot(p_ref[rows, j * BLOCK_SIZE:(j + 1) * BLOCK_SIZE], v_aug,
                        preferred_element_type=_F32)
            out = t if out is None else out + t
        acc_ref[rows, :] = alpha_ref[rows, :] * acc_ref[rows, :] + out

    lag = min(PIPELINE_LAG, g - 1)

    def stage(new_of, causal_last, old_of, descending=False):
        order = list(range(g))[::-1] if descending else list(range(g))
        for k, r in enumerate(order):
            parts = scores(r, new_of(r))
            if k < lag and old_of is not None:
                u = r if descending else g - lag + k
                pv(u, old_of(u))
            softmax(r, parts, causal_last)
        for r in order[:g - lag]:
            pv(r, new_of(r))
        return order[g - lag:]

    def past(t):
        return lambda r: [t * kb + j for j in range(kb)]

    diag = lambda r: [i * g + j for j in range(r + 1)]
    n_past = (i * g) // kb

    @pl.when(n_past == 0)
    def _():
        stage(diag, True, None, descending=True)

    @pl.when(n_past > 0)
    def _():
        stage(past(0), False, None)

        def body(t, carry):
            stage(past(t), False, past(t - 1))
            return carry

        lax.fori_loop(1, n_past, body, 0)
        stage(diag, True, past(n_past - 1), descending=True)

    for r in list(range(g))[::-1][g - lag:]:
        pv(r, diag(r))

    acc = acc_ref[...]
    o_ref[...] = (acc[:, :HEAD_DIM] / acc[:, HEAD_DIM:HEAD_DIM + 1]).astype(o_ref.dtype)


def _attention(q_aug, kt, v):
    s = q_aug.shape[0]
    nb, d, _ = kt.shape
    n_heads = d // HEAD_DIM
    tq = min(Q_TILE, s)
    g = tq // BLOCK_SIZE
    kb = min(KV_BLOCKS_PER_STEP, g)
    assert g % kb == 0 and g >= 2
    v3 = v.reshape(nb, BLOCK_SIZE, d)
    return pl.pallas_call(
        functools.partial(_attn_kernel, kb=kb),
        grid=(n_heads, s // tq),
        in_specs=[
            pl.BlockSpec((tq, AUG), lambda h, i: (i, h)),
            pl.BlockSpec((nb, HEAD_DIM, BLOCK_SIZE), lambda h, i: (0, h, 0)),
            pl.BlockSpec((nb, BLOCK_SIZE, HEAD_DIM), lambda h, i: (0, 0, h)),
        ],
        out_specs=pl.BlockSpec((tq, HEAD_DIM), lambda h, i: (i, h)),
        out_shape=jax.ShapeDtypeStruct((s, d), _BF16),
        scratch_shapes=[
            pltpu.VMEM((nb, HEAD_DIM, BLOCK_SIZE), _BF16),
            pltpu.VMEM((tq, 1), _F32),
            pltpu.VMEM((tq, 1), _F32),
            pltpu.VMEM((tq, AUG), _F32),
            pltpu.VMEM((tq, max(kb, g) * BLOCK_SIZE), _BF16),
        ],
        compiler_params=_params(2),
        name="moba_attention",
    )(q_aug, kt, v3)


def kernel(x, norm_mix, norm_ffn, ffn_w_up, ffn_conv, ffn_conv_b, ffn_w_down,
           a_w_in, a_conv, a_w_out, kv_norm, w_kv, b_w_q, b_w_o, final_norm):
    b, s, d = x.shape
    assert b == 1 and s % BLOCK_SIZE == 0 and d % MXU_WIDTH == 0
    depth = norm_mix.shape[0]
    n_a = a_w_in.shape[0]
    n_heads = d // HEAD_DIM
    nb = s // BLOCK_SIZE
    assert nb <= LANES // 2, "block bias lanes hold at most LANES/2 key blocks"

    bf = lambda w: w.astype(_BF16)
    ffn_w_up, ffn_w_down = bf(ffn_w_up), bf(ffn_w_down)
    a_w_in, a_w_out = bf(a_w_in), bf(a_w_out)
    w_kv, b_w_q, b_w_o = bf(w_kv), bf(b_w_q), bf(b_w_o)

    conv_b = ffn_conv_b.reshape(depth, 1, -1)
    xs = x.reshape(s, d)
    kt = v = km = None
    for i in range(depth):
        gmix = norm_mix[i].reshape(1, d)
        attn = j = None
        if i < n_a:
            xs = _mixer_a(xs, gmix, a_w_in, a_conv, a_w_out, i)
        else:
            j = i - n_a
            attn = _attention(_q_gate(xs, gmix, b_w_q, km, j), kt, v)
        xs = _ffn(xs, norm_ffn[i].reshape(1, d), ffn_w_up, ffn_conv, conv_b, ffn_w_down, i,
                  attn=attn, w_o=b_w_o if attn is not None else None, o_layer=j,
                  g_final=final_norm.reshape(1, d) if i == depth - 1 else None)
        if i == n_a - 1:
            kt, v, km = _shared_kv(xs, kv_norm.reshape(1, d), w_kv)
            km = km.reshape(nb, n_heads, HEAD_DIM).transpose(1, 0, 2)
            km = jnp.pad(km, ((0, 0), (0, LANES // 2 - nb), (0, 0)))
    return xs.reshape(b, s, d)
```

```python
import functools

import numpy as np
import jax
import jax.numpy as jnp
from jax import lax
from jax.experimental import pallas as pl
from jax.experimental.pallas import tpu as pltpu

HEAD_DIM = 128
BLOCK_SIZE = 256
TOP_K_BLOCKS = 3
CONV_WIDTH = 3
EPS = 1e-6
MASK_VALUE = -1e30

LANES = 128
SUBLANES = 8
MXU_WIDTH = 256
VMEM_LIMIT_BYTES = 56 * 1024 * 1024

ROW_TILE = 1024
KV_ROW_TILE = 2048
Q_TILE = 2048
KV_BLOCKS_PER_STEP = 4
PIPELINE_LAG = 6
AUG = 2 * HEAD_DIM
LOG2E = 1.4426950408889634
SPLIT = 3

_BF16 = jnp.bfloat16
_F32 = jnp.float32


def _resident(shape, layer=None):
    nd = len(shape)
    if layer is None:
        return pl.BlockSpec(shape, lambda *_: (0,) * nd, pipeline_mode=pl.Buffered(1))
    return pl.BlockSpec((None,) + tuple(shape[1:]), lambda *_: (layer,) + (0,) * (nd - 1),
                        pipeline_mode=pl.Buffered(1))


def _params(n_axes):
    return pltpu.CompilerParams(
        dimension_semantics=("arbitrary",) * n_axes,
        vmem_limit_bytes=VMEM_LIMIT_BYTES)


def _rmsnorm(x, g):
    ms = jnp.mean(x * x, axis=-1, keepdims=True)
    return x * lax.rsqrt(ms + EPS) * g


def _causal_conv3(v, prev, w):
    rows = lax.broadcasted_iota(jnp.int32, v.shape, 0)
    p1 = prev[SUBLANES - 1:SUBLANES, :]
    p2 = prev[SUBLANES - 2:SUBLANES - 1, :]
    v1 = jnp.where(rows == 0, p1, pltpu.roll(v, 1, 0))
    v2 = jnp.where(rows == 0, p2, jnp.where(rows == 1, p1, pltpu.roll(v, 2, 0)))
    return w[0:1, :] * v2 + w[1:2, :] * v1 + w[2:3, :] * v


def _mixer_a_kernel(x_ref, g_ref, win_ref, cw_ref, wout_ref, o_ref, carry_ref, y_ref):
    d = x_ref.shape[1]
    tm = x_ref.shape[0]

    @pl.when(pl.program_id(0) == 0)
    def _():
        carry_ref[...] = jnp.zeros_like(carry_ref)

    x = x_ref[...]
    h = _rmsnorm(x, g_ref[...]).astype(_BF16)

    def in_proj(c):
        return [jnp.dot(h, win_ref[:, k * d + c * MXU_WIDTH:k * d + (c + 1) * MXU_WIDTH],
                        preferred_element_type=_F32) for k in range(3)]

    for c in range(d // MXU_WIDTH):
        cs = slice(c * MXU_WIDTH, (c + 1) * MXU_WIDTH)
        bg, cg, xv = in_proj(c)
        u = cg * xv
        y = bg * _causal_conv3(u, carry_ref[:, cs], cw_ref[:, cs])
        carry_ref[:, cs] = u[tm - SUBLANES:, :]
        y_ref[:, cs] = y.astype(_BF16)
    o_ref[...] = x + jnp.dot(y_ref[...], wout_ref[...], preferred_element_type=_F32)


def _mixer_a(x, g, w_in, conv_w, w_out, layer):
    s, d = x.shape
    tm = min(ROW_TILE, s)
    return pl.pallas_call(
        _mixer_a_kernel,
        grid=(s // tm,),
        in_specs=[
            pl.BlockSpec((tm, d), lambda i: (i, 0)),
            _resident((1, d)),
            _resident(w_in.shape, layer),
            _resident(conv_w.shape, layer),
            _resident(w_out.shape, layer),
        ],
        out_specs=pl.BlockSpec((tm, d), lambda i: (i, 0)),
        out_shape=jax.ShapeDtypeStruct((s, d), _F32),
        scratch_shapes=[pltpu.VMEM((SUBLANES, d), _F32), pltpu.VMEM((tm, d), _BF16)],
        compiler_params=_params(1),
        name="mixer_a",
    )(x, g, w_in, conv_w, w_out)


def _ffn_kernel(*refs, has_proj, has_final):
    refs = list(refs)
    x_ref = refs.pop(0)
    if has_proj:
        attn_ref = refs.pop(0)
        wo_ref = refs.pop(0)
    g_ref, wup_ref, cw_ref, cb_ref, wdn_ref = refs[:5]
    refs = refs[5:]
    if has_final:
        gf_ref = refs.pop(0)
    o_ref, carry_ref, act_ref = refs

    tm = x_ref.shape[0]
    f = wdn_ref.shape[0]

    @pl.when(pl.program_id(0) == 0)
    def _():
        carry_ref[...] = jnp.zeros_like(carry_ref)

    x = x_ref[...]
    if has_proj:
        x = x + jnp.dot(attn_ref[...], wo_ref[...], preferred_element_type=_F32)
    h = _rmsnorm(x, g_ref[...]).astype(_BF16)

    def up_proj(c):
        return [jnp.dot(h, wup_ref[:, k * f + c * MXU_WIDTH:k * f + (c + 1) * MXU_WIDTH],
                        preferred_element_type=_F32) for k in range(2)]

    for c in range(f // MXU_WIDTH):
        cs = slice(c * MXU_WIDTH, (c + 1) * MXU_WIDTH)
        gp, up = up_proj(c)
        gc = _causal_conv3(gp, carry_ref[:, cs], cw_ref[:, cs]) + cb_ref[:, cs]
        carry_ref[:, cs] = gp[tm - SUBLANES:, :]
        act_ref[:, cs] = (gc * jax.nn.sigmoid(gc) * up).astype(_BF16)
    acc = x + jnp.dot(act_ref[...], wdn_ref[...], preferred_element_type=_F32)
    if has_final:
        acc = _rmsnorm(acc, gf_ref[...])
    o_ref[...] = acc


def _ffn(x, g, w_up, conv_w, conv_b, w_down, layer, attn=None, w_o=None, o_layer=None,
         g_final=None):
    s, d = x.shape
    f = w_down.shape[1]
    tm = min(ROW_TILE, s)
    row_spec = pl.BlockSpec((tm, d), lambda i: (i, 0))
    args, specs = [x], [row_spec]
    if attn is not None:
        args += [attn, w_o]
        specs += [row_spec, _resident(w_o.shape, o_layer)]
    args += [g, w_up, conv_w, conv_b, w_down]
    specs += [_resident((1, d)), _resident(w_up.shape, layer), _resident(conv_w.shape, layer),
              _resident(conv_b.shape, layer), _resident(w_down.shape, layer)]
    if g_final is not None:
        args.append(g_final)
        specs.append(_resident((1, d)))
    return pl.pallas_call(
        functools.partial(_ffn_kernel, has_proj=attn is not None,
                          has_final=g_final is not None),
        grid=(s // tm,),
        in_specs=specs,
        out_specs=row_spec,
        out_shape=jax.ShapeDtypeStruct((s, d), _F32),
        scratch_shapes=[pltpu.VMEM((SUBLANES, f), _F32), pltpu.VMEM((tm, f), _BF16)],
        compiler_params=_params(1),
        name="conv_ffn",
    )(*args)


def _kv_kernel(x_ref, g_ref, wkv_ref, kt_ref, v_ref, km_ref):
    d = x_ref.shape[1]
    nb = kt_ref.shape[0]
    h = _rmsnorm(x_ref[...], g_ref[...]).astype(_BF16)
    k = jnp.dot(h, wkv_ref[:, :d], preferred_element_type=_F32)
    v = jnp.dot(h, wkv_ref[:, d:], preferred_element_type=_F32)
    v_ref[...] = v.astype(_BF16)
    means = []
    for b in range(nb):
        kb = k[b * BLOCK_SIZE:(b + 1) * BLOCK_SIZE, :]
        means.append(jnp.mean(kb, axis=0, keepdims=True))
        kt_ref[b] = kb.T.astype(_BF16)
    km_ref[...] = jnp.concatenate(means, axis=0)


def _shared_kv(x, g, w_kv):
    s, d = x.shape
    tm = min(KV_ROW_TILE, s)
    nb_step = tm // BLOCK_SIZE
    nb = s // BLOCK_SIZE
    return pl.pallas_call(
        _kv_kernel,
        grid=(s // tm,),
        in_specs=[
            pl.BlockSpec((tm, d), lambda i: (i, 0)),
            _resident((1, d)),
            _resident(w_kv.shape),
        ],
        out_specs=[
            pl.BlockSpec((nb_step, d, BLOCK_SIZE), lambda i: (i, 0, 0)),
            pl.BlockSpec((tm, d), lambda i: (i, 0)),
            pl.BlockSpec((nb_step, d), lambda i: (i, 0)),
        ],
        out_shape=[
            jax.ShapeDtypeStruct((nb, d, BLOCK_SIZE), _BF16),
            jax.ShapeDtypeStruct((s, d), _BF16),
            jax.ShapeDtypeStruct((nb, d), _F32),
        ],
        compiler_params=_params(1),
        name="shared_kv",
    )(x, g, w_kv)


def _bf16_pieces(x):
    out = []
    for _ in range(SPLIT):
        piece = float(np.float32(x).astype(_BF16))
        out.append(piece)
        x -= piece
    return out


def _q_gate_kernel(x_ref, g_ref, wq_ref, km_ref, o_ref):
    tm, d = x_ref.shape
    n_heads = d // HEAD_DIM
    scale = HEAD_DIM ** -0.5
    h = _rmsnorm(x_ref[...], g_ref[...]).astype(_BF16)
    q = jnp.dot(h, wq_ref[...], preferred_element_type=_F32)

    nbl = LANES // 2
    pos = pl.program_id(0) * tm + lax.broadcasted_iota(jnp.int32, (1, tm), 1)
    own = lax.shift_right_logical(pos, BLOCK_SIZE.bit_length() - 1)
    blk = lax.broadcasted_iota(jnp.int32, (nbl, tm), 0)
    valid = blk < own
    tail_row = lax.broadcasted_iota(jnp.int32, (LANES - nbl, tm), 0)

    for hd in range(n_heads):
        qh = q[:, hd * HEAD_DIM:(hd + 1) * HEAD_DIM]
        gate = lax.dot_general(km_ref[hd], qh, (((1,), (1,)), ((), ())),
                               preferred_element_type=_F32,
                               precision=lax.Precision.HIGHEST)
        gate = jnp.where(valid, gate, -jnp.inf)
        keep = blk == own
        for _ in range(TOP_K_BLOCKS):
            m = jnp.max(gate, axis=0, keepdims=True)
            first = jnp.min(jnp.where(gate == m, blk, nbl), axis=0, keepdims=True)
            pick = (blk == first) & (m > -jnp.inf)
            keep = keep | pick
            gate = jnp.where(pick, -jnp.inf, gate)
        bias = jnp.where(keep, 0.0, MASK_VALUE)
        slope = 2.0 ** (-8.0 * (hd + 1) / n_heads)
        tail = jnp.zeros((LANES - nbl, tm), _F32)
        for k, piece in enumerate(_bf16_pieces(slope * LOG2E)):
            tail = jnp.where(tail_row == k, piece * BLOCK_SIZE, tail)
            tail = jnp.where(tail_row == SPLIT + k, piece, tail)
        aux = jnp.concatenate([bias, tail], axis=0).T
        o_ref[:, hd * AUG:hd * AUG + HEAD_DIM] = (qh * (scale * LOG2E)).astype(_BF16)
        o_ref[:, hd * AUG + HEAD_DIM:(hd + 1) * AUG] = aux.astype(_BF16)


def _q_gate(x, g, w_q, km, layer):
    s, d = x.shape
    n_heads = d // HEAD_DIM
    tm = min(ROW_TILE, s)
    return pl.pallas_call(
        _q_gate_kernel,
        grid=(s // tm,),
        in_specs=[
            pl.BlockSpec((tm, d), lambda i: (i, 0)),
            _resident((1, d)),
            _resident(w_q.shape, layer),
            _resident(km.shape),
        ],
        out_specs=pl.BlockSpec((tm, n_heads * AUG), lambda i: (i, 0)),
        out_shape=jax.ShapeDtypeStruct((s, n_heads * AUG), _BF16),
        compiler_params=_params(1),
        name="q_gate",
    )(x, g, w_q, km)


def _attn_kernel(q_ref, kt_ref, v_ref, o_ref, kaux_ref, m_ref, alpha_ref, acc_ref, p_ref,
                 *, kb):
    tq = q_ref.shape[0]
    nb = kt_ref.shape[0]
    g = tq // BLOCK_SIZE
    i = pl.program_id(1)

    @pl.when((pl.program_id(0) == 0) & (i == 0))
    def _():
        nbl = LANES // 2
        row = lax.broadcasted_iota(jnp.int32, (HEAD_DIM, BLOCK_SIZE), 0)
        off = lax.broadcasted_iota(jnp.int32, (HEAD_DIM, BLOCK_SIZE), 1).astype(_F32)
        is_off = (row >= nbl + SPLIT) & (row < nbl + 2 * SPLIT)
        is_id = (row >= nbl) & (row < nbl + SPLIT)
        base = jnp.where(is_off, off, 0.0)
        for n in range(nb):
            kaux_ref[n] = jnp.where(row == n, 1.0,
                                    jnp.where(is_id, float(n), base)).astype(_BF16)

    lane = lax.broadcasted_iota(jnp.int32, (BLOCK_SIZE, HEAD_DIM), 1)
    ones_col = jnp.where(lane == 0, 1.0, 0.0).astype(_BF16)

    m_ref[...] = jnp.full_like(m_ref, MASK_VALUE)
    acc_ref[...] = jnp.zeros_like(acc_ref)

    def scores(r, blocks):
        q = q_ref[r * BLOCK_SIZE:(r + 1) * BLOCK_SIZE, :]
        return [jnp.dot(q, jnp.concatenate([kt_ref[n], kaux_ref[n]], axis=0),
                        preferred_element_type=_F32) for n in blocks]

    def softmax(r, parts, causal_last):
        rows = slice(r * BLOCK_SIZE, (r + 1) * BLOCK_SIZE)
        if causal_last:
            qpos = lax.broadcasted_iota(jnp.int32, (BLOCK_SIZE, BLOCK_SIZE), 0)
            kpos = lax.broadcasted_iota(jnp.int32, (BLOCK_SIZE, BLOCK_SIZE), 1)
            parts[-1] = jnp.where(kpos <= qpos, parts[-1], MASK_VALUE)
        s = parts[0] if len(parts) == 1 else jnp.concatenate(parts, axis=1)
        m_prev = m_ref[rows, :]
        m_new = jnp.maximum(m_prev, jnp.max(s, axis=-1, keepdims=True))
        alpha_ref[rows, :] = jnp.exp2(m_prev - m_new)
        p_ref[rows, :s.shape[1]] = jnp.exp2(s - m_new).astype(_BF16)
        m_ref[rows, :] = m_new

    def pv(r, blocks):
        rows = slice(r * BLOCK_SIZE, (r + 1) * BLOCK_SIZE)
        out = None
        for j, n in enumerate(blocks):
            v_aug = jnp.concatenate([v_ref[n], ones_col], axis=1)
            t = jnp.dot(p_ref[rows, j * BLOCK_SIZE:(j + 1) * BLOCK_SIZE], v_aug,
                        preferred_element_type=_F32)
            out = t if out is None else out + t
        acc_ref[rows, :] = alpha_ref[rows, :] * acc_ref[rows, :] + out

    lag = min(PIPELINE_LAG, g - 1)

    def stage(new_of, causal_last, old_of, descending=False):
        order = list(range(g))[::-1] if descending else list(range(g))
        for k, r in enumerate(order):
            parts = scores(r, new_of(r))
            if k < lag and old_of is not None:
                u = r if descending else g - lag + k
                pv(u, old_of(u))
            softmax(r, parts, causal_last)
        for r in order[:g - lag]:
            pv(r, new_of(r))
        return order[g - lag:]

    def past(t):
        return lambda r: [t * kb + j for j in range(kb)]

    diag = lambda r: [i * g + j for j in range(r + 1)]
    n_past = (i * g) // kb

    @pl.when(n_past == 0)
    def _():
        stage(diag, True, None, descending=True)

    @pl.when(n_past > 0)
    def _():
        stage(past(0), False, None)

        def body(t, carry):
            stage(past(t), False, past(t - 1))
            return carry

        lax.fori_loop(1, n_past, body, 0)
        stage(diag, True, past(n_past - 1), descending=True)

    for r in list(range(g))[::-1][g - lag:]:
        pv(r, diag(r))

    acc = acc_ref[...]
    o_ref[...] = (acc[:, :HEAD_DIM] / acc[:, HEAD_DIM:HEAD_DIM + 1]).astype(o_ref.dtype)


def _attention(q_aug, kt, v):
    s = q_aug.shape[0]
    nb, d, _ = kt.shape
    n_heads = d // HEAD_DIM
    tq = min(Q_TILE, s)
    g = tq // BLOCK_SIZE
    kb = min(KV_BLOCKS_PER_STEP, g)
    assert g % kb == 0 and g >= 2
    v3 = v.reshape(nb, BLOCK_SIZE, d)
    return pl.pallas_call(
        functools.partial(_attn_kernel, kb=kb),
        grid=(n_heads, s // tq),
        in_specs=[
            pl.BlockSpec((tq, AUG), lambda h, i: (i, h)),
            pl.BlockSpec((nb, HEAD_DIM, BLOCK_SIZE), lambda h, i: (0, h, 0)),
            pl.BlockSpec((nb, BLOCK_SIZE, HEAD_DIM), lambda h, i: (0, 0, h)),
        ],
        out_specs=pl.BlockSpec((tq, HEAD_DIM), lambda h, i: (i, h)),
        out_shape=jax.ShapeDtypeStruct((s, d), _BF16),
        scratch_shapes=[
            pltpu.VMEM((nb, HEAD_DIM, BLOCK_SIZE), _BF16),
            pltpu.VMEM((tq, 1), _F32),
            pltpu.VMEM((tq, 1), _F32),
            pltpu.VMEM((tq, AUG), _F32),
            pltpu.VMEM((tq, max(kb, g) * BLOCK_SIZE), _BF16),
        ],
        compiler_params=_params(2),
        name="moba_attention",
    )(q_aug, kt, v3)


def kernel(x, norm_mix, norm_ffn, ffn_w_up, ffn_conv, ffn_conv_b, ffn_w_down,
           a_w_in, a_conv, a_w_out, kv_norm, w_kv, b_w_q, b_w_o, final_norm):
    b, s, d = x.shape
    assert b == 1 and s % BLOCK_SIZE == 0 and d % MXU_WIDTH == 0
    depth = norm_mix.shape[0]
    n_a = a_w_in.shape[0]
    n_heads = d // HEAD_DIM
    nb = s // BLOCK_SIZE
    assert nb <= LANES // 2, "block bias lanes hold at most LANES/2 key blocks"

    bf = lambda w: w.astype(_BF16)
    ffn_w_up, ffn_w_down = bf(ffn_w_up), bf(ffn_w_down)
    a_w_in, a_w_out = bf(a_w_in), bf(a_w_out)
    w_kv, b_w_q, b_w_o = bf(w_kv), bf(b_w_q), bf(b_w_o)

    conv_b = ffn_conv_b.reshape(depth, 1, -1)
    xs = x.reshape(s, d)
    kt = v = km = None
    for i in range(depth):
        gmix = norm_mix[i].reshape(1, d)
        attn = j = None
        if i < n_a:
            xs = _mixer_a(xs, gmix, a_w_in, a_conv, a_w_out, i)
        else:
            j = i - n_a
            attn = _attention(_q_gate(xs, gmix, b_w_q, km, j), kt, v)
        xs = _ffn(xs, norm_ffn[i].reshape(1, d), ffn_w_up, ffn_conv, conv_b, ffn_w_down, i,
                  attn=attn, w_o=b_w_o if attn is not None else None, o_layer=j,
                  g_final=final_norm.reshape(1, d) if i == depth - 1 else None)
        if i == n_a - 1:
            kt, v, km = _shared_kv(xs, kv_norm.reshape(1, d), w_kv)
            km = km.reshape(nb, n_heads, HEAD_DIM).transpose(1, 0, 2)
            km = jnp.pad(km, ((0, 0), (0, LANES // 2 - nb), (0, 0)))
    return xs.reshape(b, s, d)
```

```python
import functools

import numpy as np
import jax
import jax.numpy as jnp
from jax import lax
from jax.experimental import pallas as pl
from jax.experimental.pallas import tpu as pltpu

HEAD_DIM = 128
BLOCK_SIZE = 256
TOP_K_BLOCKS = 3
EPS = 1e-6
MASK_VALUE = -1e30

LANES = 128
SUBLANES = 8
MXU_WIDTH = 256
VMEM_LIMIT_BYTES = 56 * 1024 * 1024

ROW_TILE = 1024
KV_ROW_TILE = 2048
Q_TILE = 2048
KV_BLOCKS_PER_STEP = 4
PIPELINE_LAG = 6
AUG = 2 * HEAD_DIM
MASK_LANES = LANES // 2
LOG2E = 1.4426950408889634
SPLIT = 3

_BF16 = jnp.bfloat16
_F32 = jnp.float32


def _resident(shape, layer=None):
    nd = len(shape)
    if layer is None:
        return pl.BlockSpec(shape, lambda *_: (0,) * nd, pipeline_mode=pl.Buffered(1))
    return pl.BlockSpec((None,) + tuple(shape[1:]), lambda *_: (layer,) + (0,) * (nd - 1),
                        pipeline_mode=pl.Buffered(1))


def _params(n_axes):
    return pltpu.CompilerParams(
        dimension_semantics=("arbitrary",) * n_axes,
        vmem_limit_bytes=VMEM_LIMIT_BYTES)


def _rmsnorm(x, g):
    ms = jnp.mean(x * x, axis=-1, keepdims=True)
    return x * lax.rsqrt(ms + EPS) * g


def _causal_conv3(v, prev, w):
    rows = lax.broadcasted_iota(jnp.int32, v.shape, 0)
    p1 = prev[SUBLANES - 1:SUBLANES, :]
    p2 = prev[SUBLANES - 2:SUBLANES - 1, :]
    v1 = jnp.where(rows == 0, p1, pltpu.roll(v, 1, 0))
    v2 = jnp.where(rows == 0, p2, jnp.where(rows == 1, p1, pltpu.roll(v, 2, 0)))
    return w[0:1, :] * v2 + w[1:2, :] * v1 + w[2:3, :] * v


def _mixer_a_kernel(x_ref, g_ref, win_ref, cw_ref, wout_ref, o_ref, carry_ref, y_ref):
    d = x_ref.shape[1]
    tm = x_ref.shape[0]

    @pl.when(pl.program_id(0) == 0)
    def _():
        carry_ref[...] = jnp.zeros_like(carry_ref)

    x = x_ref[...]
    h = _rmsnorm(x, g_ref[...]).astype(_BF16)

    def in_proj(c):
        return [jnp.dot(h, win_ref[:, k * d + c * MXU_WIDTH:k * d + (c + 1) * MXU_WIDTH],
                        preferred_element_type=_F32) for k in range(3)]

    for c in range(d // MXU_WIDTH):
        cs = slice(c * MXU_WIDTH, (c + 1) * MXU_WIDTH)
        bg, cg, xv = in_proj(c)
        u = cg * xv
        y = bg * _causal_conv3(u, carry_ref[:, cs], cw_ref[:, cs])
        carry_ref[:, cs] = u[tm - SUBLANES:, :]
        y_ref[:, cs] = y.astype(_BF16)
    o_ref[...] = x + jnp.dot(y_ref[...], wout_ref[...], preferred_element_type=_F32)


def _mixer_a(x, g, w_in, conv_w, w_out, layer):
    s, d = x.shape
    tm = min(ROW_TILE, s)
    return pl.pallas_call(
        _mixer_a_kernel,
        grid=(s // tm,),
        in_specs=[
            pl.BlockSpec((tm, d), lambda i: (i, 0)),
            _resident((1, d)),
            _resident(w_in.shape, layer),
            _resident(conv_w.shape, layer),
            _resident(w_out.shape, layer),
        ],
        out_specs=pl.BlockSpec((tm, d), lambda i: (i, 0)),
        out_shape=jax.ShapeDtypeStruct((s, d), _F32),
        scratch_shapes=[pltpu.VMEM((SUBLANES, d), _F32), pltpu.VMEM((tm, d), _BF16)],
        compiler_params=_params(1),
        name="mixer_a",
    )(x, g, w_in, conv_w, w_out)


def _ffn_kernel(*refs, has_proj, has_final):
    refs = list(refs)
    x_ref = refs.pop(0)
    if has_proj:
        attn_ref = refs.pop(0)
        wo_ref = refs.pop(0)
    g_ref, wup_ref, cw_ref, cb_ref, wdn_ref = refs[:5]
    refs = refs[5:]
    if has_final:
        gf_ref = refs.pop(0)
    o_ref, carry_ref, act_ref = refs

    tm = x_ref.shape[0]
    f = wdn_ref.shape[0]

    @pl.when(pl.program_id(0) == 0)
    def _():
        carry_ref[...] = jnp.zeros_like(carry_ref)

    x = x_ref[...]
    if has_proj:
        x = x + jnp.dot(attn_ref[...], wo_ref[...], preferred_element_type=_F32)
    h = _rmsnorm(x, g_ref[...]).astype(_BF16)

    def up_proj(c):
        return [jnp.dot(h, wup_ref[:, k * f + c * MXU_WIDTH:k * f + (c + 1) * MXU_WIDTH],
                        preferred_element_type=_F32) for k in range(2)]

    for c in range(f // MXU_WIDTH):
        cs = slice(c * MXU_WIDTH, (c + 1) * MXU_WIDTH)
        gp, up = up_proj(c)
        gc = _causal_conv3(gp, carry_ref[:, cs], cw_ref[:, cs]) + cb_ref[:, cs]
        carry_ref[:, cs] = gp[tm - SUBLANES:, :]
        act_ref[:, cs] = (gc * jax.nn.sigmoid(gc) * up).astype(_BF16)
    acc = x + jnp.dot(act_ref[...], wdn_ref[...], preferred_element_type=_F32)
    if has_final:
        acc = _rmsnorm(acc, gf_ref[...])
    o_ref[...] = acc


def _ffn(x, g, w_up, conv_w, conv_b, w_down, layer, attn=None, w_o=None, o_layer=None,
         g_final=None):
    s, d = x.shape
    f = w_down.shape[1]
    tm = min(ROW_TILE, s)
    row_spec = pl.BlockSpec((tm, d), lambda i: (i, 0))
    args, specs = [x], [row_spec]
    if attn is not None:
        args += [attn, w_o]
        specs += [row_spec, _resident(w_o.shape, o_layer)]
    args += [g, w_up, conv_w, conv_b, w_down]
    specs += [_resident((1, d)), _resident(w_up.shape, layer), _resident(conv_w.shape, layer),
              _resident(conv_b.shape, layer), _resident(w_down.shape, layer)]
    if g_final is not None:
        args.append(g_final)
        specs.append(_resident((1, d)))
    return pl.pallas_call(
        functools.partial(_ffn_kernel, has_proj=attn is not None,
                          has_final=g_final is not None),
        grid=(s // tm,),
        in_specs=specs,
        out_specs=row_spec,
        out_shape=jax.ShapeDtypeStruct((s, d), _F32),
        scratch_shapes=[pltpu.VMEM((SUBLANES, f), _F32), pltpu.VMEM((tm, f), _BF16)],
        compiler_params=_params(1),
        name="conv_ffn",
    )(*args)


def _kv_kernel(x_ref, g_ref, wkv_ref, kt_ref, v_ref, km_ref):
    d = x_ref.shape[1]
    nb = kt_ref.shape[0]
    h = _rmsnorm(x_ref[...], g_ref[...]).astype(_BF16)
    k = jnp.dot(h, wkv_ref[:, :d], preferred_element_type=_F32)
    v = jnp.dot(h, wkv_ref[:, d:], preferred_element_type=_F32)
    v_ref[...] = v.astype(_BF16)
    means = []
    for b in range(nb):
        kb = k[b * BLOCK_SIZE:(b + 1) * BLOCK_SIZE, :]
        means.append(jnp.mean(kb, axis=0, keepdims=True))
        kt_ref[b] = kb.T.astype(_BF16)
    km_ref[...] = jnp.concatenate(means, axis=0)


def _shared_kv(x, g, w_kv):
    s, d = x.shape
    tm = min(KV_ROW_TILE, s)
    nb_step = tm // BLOCK_SIZE
    nb = s // BLOCK_SIZE
    return pl.pallas_call(
        _kv_kernel,
        grid=(s // tm,),
        in_specs=[
            pl.BlockSpec((tm, d), lambda i: (i, 0)),
            _resident((1, d)),
            _resident(w_kv.shape),
        ],
        out_specs=[
            pl.BlockSpec((nb_step, d, BLOCK_SIZE), lambda i: (i, 0, 0)),
            pl.BlockSpec((tm, d), lambda i: (i, 0)),
            pl.BlockSpec((nb_step, d), lambda i: (i, 0)),
        ],
        out_shape=[
            jax.ShapeDtypeStruct((nb, d, BLOCK_SIZE), _BF16),
            jax.ShapeDtypeStruct((s, d), _BF16),
            jax.ShapeDtypeStruct((nb, d), _F32),
        ],
        compiler_params=_params(1),
        name="shared_kv",
    )(x, g, w_kv)


def _bf16_pieces(x):
    out = []
    for _ in range(SPLIT):
        piece = float(np.float32(x).astype(_BF16))
        out.append(piece)
        x -= piece
    return out


def _q_gate_kernel(x_ref, g_ref, wq_ref, km_ref, o_ref):
    tm, d = x_ref.shape
    n_heads = d // HEAD_DIM
    scale = HEAD_DIM ** -0.5
    h = _rmsnorm(x_ref[...], g_ref[...]).astype(_BF16)
    q = jnp.dot(h, wq_ref[...], preferred_element_type=_F32)

    nbl = MASK_LANES
    pos = pl.program_id(0) * tm + lax.broadcasted_iota(jnp.int32, (1, tm), 1)
    own = lax.shift_right_logical(pos, BLOCK_SIZE.bit_length() - 1)
    blk = lax.broadcasted_iota(jnp.int32, (nbl, tm), 0)
    valid = blk < own
    tail_row = lax.broadcasted_iota(jnp.int32, (LANES - nbl, tm), 0)

    for hd in range(n_heads):
        qh = q[:, hd * HEAD_DIM:(hd + 1) * HEAD_DIM]
        gate = lax.dot_general(km_ref[hd], qh, (((1,), (1,)), ((), ())),
                               preferred_element_type=_F32,
                               precision=lax.Precision.HIGHEST)
        gate = jnp.where(valid, gate, -jnp.inf)
        keep = blk == own
        for _ in range(TOP_K_BLOCKS):
            m = jnp.max(gate, axis=0, keepdims=True)
            first = jnp.min(jnp.where(gate == m, blk, nbl), axis=0, keepdims=True)
            pick = (blk == first) & (m > -jnp.inf)
            keep = keep | pick
            gate = jnp.where(pick, -jnp.inf, gate)
        bias = jnp.where(keep, 0.0, MASK_VALUE)
        slope = 2.0 ** (-8.0 * (hd + 1) / n_heads)
        tail = jnp.zeros((LANES - nbl, tm), _F32)
        for k, piece in enumerate(_bf16_pieces(slope * LOG2E)):
            tail = jnp.where(tail_row == k, piece * BLOCK_SIZE, tail)
            tail = jnp.where(tail_row == SPLIT + k, piece, tail)
        aux = jnp.concatenate([bias, tail], axis=0).T
        o_ref[:, hd * AUG:hd * AUG + HEAD_DIM] = (qh * (scale * LOG2E)).astype(_BF16)
        o_ref[:, hd * AUG + HEAD_DIM:(hd + 1) * AUG] = aux.astype(_BF16)


def _q_gate(x, g, w_q, km, layer):
    s, d = x.shape
    n_heads = d // HEAD_DIM
    tm = min(ROW_TILE, s)
    return pl.pallas_call(
        _q_gate_kernel,
        grid=(s // tm,),
        in_specs=[
            pl.BlockSpec((tm, d), lambda i: (i, 0)),
            _resident((1, d)),
            _resident(w_q.shape, layer),
            _resident(km.shape),
        ],
        out_specs=pl.BlockSpec((tm, n_heads * AUG), lambda i: (i, 0)),
        out_shape=jax.ShapeDtypeStruct((s, n_heads * AUG), _BF16),
        compiler_params=_params(1),
        name="q_gate",
    )(x, g, w_q, km)


def _attn_kernel(q_ref, kt_ref, v_ref, o_ref, kaux_ref, m_ref, alpha_ref, acc_ref, p_ref,
                 *, kb):
    tq = q_ref.shape[0]
    nb = kt_ref.shape[0]
    g = tq // BLOCK_SIZE
    i = pl.program_id(1)

    @pl.when((pl.program_id(0) == 0) & (i == 0))
    def _():
        nbl = MASK_LANES
        row = lax.broadcasted_iota(jnp.int32, (HEAD_DIM, BLOCK_SIZE), 0)
        off = lax.broadcasted_iota(jnp.int32, (HEAD_DIM, BLOCK_SIZE), 1).astype(_F32)
        is_off = (row >= nbl + SPLIT) & (row < nbl + 2 * SPLIT)
        is_id = (row >= nbl) & (row < nbl + SPLIT)
        base = jnp.where(is_off, off, 0.0)
        for n in range(nb):
            kaux_ref[n] = jnp.where(row == n, 1.0,
                                    jnp.where(is_id, float(n), base)).astype(_BF16)

    lane = lax.broadcasted_iota(jnp.int32, (BLOCK_SIZE, HEAD_DIM), 1)
    ones_col = jnp.where(lane == 0, 1.0, 0.0).astype(_BF16)

    m_ref[...] = jnp.full_like(m_ref, MASK_VALUE)
    acc_ref[...] = jnp.zeros_like(acc_ref)

    def scores(r, blocks):
        q = q_ref[r * BLOCK_SIZE:(r + 1) * BLOCK_SIZE, :]
        return [jnp.dot(q, jnp.concatenate([kt_ref[n], kaux_ref[n]], axis=0),
                        preferred_element_type=_F32) for n in blocks]

    def softmax(r, parts, causal_last):
        rows = slice(r * BLOCK_SIZE, (r + 1) * BLOCK_SIZE)
        if causal_last:
            qpos = lax.broadcasted_iota(jnp.int32, (BLOCK_SIZE, BLOCK_SIZE), 0)
            kpos = lax.broadcasted_iota(jnp.int32, (BLOCK_SIZE, BLOCK_SIZE), 1)
            parts[-1] = jnp.where(kpos <= qpos, parts[-1], MASK_VALUE)
        s = parts[0] if len(parts) == 1 else jnp.concatenate(parts, axis=1)
        m_prev = m_ref[rows, :]
        m_new = jnp.maximum(m_prev, jnp.max(s, axis=-1, keepdims=True))
        alpha_ref[rows, :] = jnp.exp2(m_prev - m_new)
        p_ref[rows, :s.shape[1]] = jnp.exp2(s - m_new).astype(_BF16)
        m_ref[rows, :] = m_new

    def pv(r, blocks):
        rows = slice(r * BLOCK_SIZE, (r + 1) * BLOCK_SIZE)
        out = None
        for j, n in enumerate(blocks):
            v_aug = jnp.concatenate([v_ref[n], ones_col], axis=1)
            t = jnp.dot(p_ref[rows, j * BLOCK_SIZE:(j + 1) * BLOCK_SIZE], v_aug,
                        preferred_element_type=_F32)
            out = t if out is None else out + t
        acc_ref[rows, :] = alpha_ref[rows, :] * acc_ref[rows, :] + out

    lag = min(PIPELINE_LAG, g - 1)

    def stage(new_of, causal_last, old_of, descending=False):
        order = list(range(g))[::-1] if descending else list(range(g))
        for k, r in enumerate(order):
            parts = scores(r, new_of(r))
            if k < lag and old_of is not None:
                u = r if descending else g - lag + k
                pv(u, old_of(u))
            softmax(r, parts, causal_last)
        for r in order[:g - lag]:
            pv(r, new_of(r))
        return order[g - lag:]

    def past(t):
        return lambda r: [t * kb + j for j in range(kb)]

    diag = lambda r: [i * g + j for j in range(r + 1)]
    n_past = (i * g) // kb

    @pl.when(n_past == 0)
    def _():
        stage(diag, True, None, descending=True)

    @pl.when(n_past > 0)
    def _():
        stage(past(0), False, None)

        def body(t, carry):
            stage(past(t), False, past(t - 1))
            return carry

        lax.fori_loop(1, n_past, body, 0)
        stage(diag, True, past(n_past - 1), descending=True)

    for r in list(range(g))[::-1][g - lag:]:
        pv(r, diag(r))

    acc = acc_ref[...]
    o_ref[...] = (acc[:, :HEAD_DIM] / acc[:, HEAD_DIM:HEAD_DIM + 1]).astype(o_ref.dtype)


def _attention(q_aug, kt, v):
    s = q_aug.shape[0]
    nb, d, _ = kt.shape
    n_heads = d // HEAD_DIM
    tq = min(Q_TILE, s)
    g = tq // BLOCK_SIZE
    kb = min(KV_BLOCKS_PER_STEP, g)
    assert g % kb == 0 and g >= 2
    v3 = v.reshape(nb, BLOCK_SIZE, d)
    return pl.pallas_call(
        functools.partial(_attn_kernel, kb=kb),
        grid=(n_heads, s // tq),
        in_specs=[
            pl.BlockSpec((tq, AUG), lambda h, i: (i, h)),
            pl.BlockSpec((nb, HEAD_DIM, BLOCK_SIZE), lambda h, i: (0, h, 0)),
            pl.BlockSpec((nb, BLOCK_SIZE, HEAD_DIM), lambda h, i: (0, 0, h)),
        ],
        out_specs=pl.BlockSpec((tq, HEAD_DIM), lambda h, i: (i, h)),
        out_shape=jax.ShapeDtypeStruct((s, d), _BF16),
        scratch_shapes=[
            pltpu.VMEM((nb, HEAD_DIM, BLOCK_SIZE), _BF16),
            pltpu.VMEM((tq, 1), _F32),
            pltpu.VMEM((tq, 1), _F32),
            pltpu.VMEM((tq, AUG), _F32),
            pltpu.VMEM((tq, max(kb, g) * BLOCK_SIZE), _BF16),
        ],
        compiler_params=_params(2),
        name="moba_attention",
    )(q_aug, kt, v3)


def kernel(x, norm_mix, norm_ffn, ffn_w_up, ffn_conv, ffn_conv_b, ffn_w_down,
           a_w_in, a_conv, a_w_out, kv_norm, w_kv, b_w_q, b_w_o, final_norm):
    b, s, d = x.shape
    assert b == 1 and s % BLOCK_SIZE == 0 and d % MXU_WIDTH == 0
    depth = norm_mix.shape[0]
    n_a = a_w_in.shape[0]
    n_heads = d // HEAD_DIM
    nb = s // BLOCK_SIZE
    assert nb <= MASK_LANES, "one mask lane per key block"

    bf = lambda w: w.astype(_BF16)
    ffn_w_up, ffn_w_down = bf(ffn_w_up), bf(ffn_w_down)
    a_w_in, a_w_out = bf(a_w_in), bf(a_w_out)
    w_kv, b_w_q, b_w_o = bf(w_kv), bf(b_w_q), bf(b_w_o)

    conv_b = ffn_conv_b.reshape(depth, 1, -1)
    xs = x.reshape(s, d)
    kt = v = km = None
    for i in range(depth):
        gmix = norm_mix[i].reshape(1, d)
        attn = j = None
        if i < n_a:
            xs = _mixer_a(xs, gmix, a_w_in, a_conv, a_w_out, i)
        else:
            j = i - n_a
            attn = _attention(_q_gate(xs, gmix, b_w_q, km, j), kt, v)
        xs = _ffn(xs, norm_ffn[i].reshape(1, d), ffn_w_up, ffn_conv, conv_b, ffn_w_down, i,
                  attn=attn, w_o=b_w_o if attn is not None else None, o_layer=j,
                  g_final=final_norm.reshape(1, d) if i == depth - 1 else None)
        if i == n_a - 1:
            kt, v, km = _shared_kv(xs, kv_norm.reshape(1, d), w_kv)
            km = km.reshape(nb, n_heads, HEAD_DIM).transpose(1, 0, 2)
            km = jnp.pad(km, ((0, 0), (0, MASK_LANES - nb), (0, 0)))
    return xs.reshape(b, s, d)
```

```python
import functools

import numpy as np
import jax
import jax.numpy as jnp
from jax import lax
from jax.experimental import pallas as pl
from jax.experimental.pallas import tpu as pltpu

HEAD_DIM = 128
BLOCK_SIZE = 256
TOP_K_BLOCKS = 3
EPS = 1e-6
MASK_VALUE = -1e30

LANES = 128
SUBLANES = 8
MXU_WIDTH = 256
VMEM_LIMIT_BYTES = 56 * 1024 * 1024

ROW_TILE = 1024
KV_ROW_TILE = 2048
Q_TILE = 2048
KV_BLOCKS_PER_STEP = 4
PIPELINE_LAG = 6
AUG = 2 * HEAD_DIM
MASK_LANES = LANES // 2
LOG2E = 1.4426950408889634
SPLIT = 3

_BF16 = jnp.bfloat16
_F32 = jnp.float32


def _resident(shape, layer=None):
    nd = len(shape)
    if layer is None:
        return pl.BlockSpec(shape, lambda *_: (0,) * nd, pipeline_mode=pl.Buffered(1))
    return pl.BlockSpec((None,) + tuple(shape[1:]), lambda *_: (layer,) + (0,) * (nd - 1),
                        pipeline_mode=pl.Buffered(1))


def _params(n_axes):
    return pltpu.CompilerParams(
        dimension_semantics=("arbitrary",) * n_axes,
        vmem_limit_bytes=VMEM_LIMIT_BYTES)


def _rmsnorm(x, g):
    ms = jnp.mean(x * x, axis=-1, keepdims=True)
    return x * lax.rsqrt(ms + EPS) * g


def _causal_conv3(v, prev, w):
    rows = lax.broadcasted_iota(jnp.int32, v.shape, 0)
    p1 = prev[SUBLANES - 1:SUBLANES, :]
    p2 = prev[SUBLANES - 2:SUBLANES - 1, :]
    v1 = jnp.where(rows == 0, p1, pltpu.roll(v, 1, 0))
    v2 = jnp.where(rows == 0, p2, jnp.where(rows == 1, p1, pltpu.roll(v, 2, 0)))
    return w[0:1, :] * v2 + w[1:2, :] * v1 + w[2:3, :] * v


def _mixer_a_kernel(x_ref, g_ref, win_ref, cw_ref, wout_ref, o_ref, carry_ref, y_ref):
    d = x_ref.shape[1]
    tm = x_ref.shape[0]

    @pl.when(pl.program_id(0) == 0)
    def _():
        carry_ref[...] = jnp.zeros_like(carry_ref)

    x = x_ref[...]
    h = _rmsnorm(x, g_ref[...]).astype(_BF16)

    def in_proj(c):
        return [jnp.dot(h, win_ref[:, k * d + c * MXU_WIDTH:k * d + (c + 1) * MXU_WIDTH],
                        preferred_element_type=_F32) for k in range(3)]

    for c in range(d // MXU_WIDTH):
        cs = slice(c * MXU_WIDTH, (c + 1) * MXU_WIDTH)
        bg, cg, xv = in_proj(c)
        u = cg * xv
        y = bg * _causal_conv3(u, carry_ref[:, cs], cw_ref[:, cs])
        carry_ref[:, cs] = u[tm - SUBLANES:, :]
        y_ref[:, cs] = y.astype(_BF16)
    o_ref[...] = x + jnp.dot(y_ref[...], wout_ref[...], preferred_element_type=_F32)


def _mixer_a(x, g, w_in, conv_w, w_out, layer):
    s, d = x.shape
    tm = min(ROW_TILE, s)
    return pl.pallas_call(
        _mixer_a_kernel,
        grid=(s // tm,),
        in_specs=[
            pl.BlockSpec((tm, d), lambda i: (i, 0)),
            _resident((1, d)),
            _resident(w_in.shape, layer),
            _resident(conv_w.shape, layer),
            _resident(w_out.shape, layer),
        ],
        out_specs=pl.BlockSpec((tm, d), lambda i: (i, 0)),
        out_shape=jax.ShapeDtypeStruct((s, d), _F32),
        scratch_shapes=[pltpu.VMEM((SUBLANES, d), _F32), pltpu.VMEM((tm, d), _BF16)],
        compiler_params=_params(1),
        name="mixer_a",
    )(x, g, w_in, conv_w, w_out)


def _ffn_kernel(*refs, has_proj, has_final):
    refs = list(refs)
    x_ref = refs.pop(0)
    if has_proj:
        attn_ref = refs.pop(0)
        wo_ref = refs.pop(0)
    g_ref, wup_ref, cw_ref, cb_ref, wdn_ref = refs[:5]
    refs = refs[5:]
    if has_final:
        gf_ref = refs.pop(0)
    o_ref, carry_ref, act_ref = refs

    tm = x_ref.shape[0]
    f = wdn_ref.shape[0]

    @pl.when(pl.program_id(0) == 0)
    def _():
        carry_ref[...] = jnp.zeros_like(carry_ref)

    x = x_ref[...]
    if has_proj:
        x = x + jnp.dot(attn_ref[...], wo_ref[...], preferred_element_type=_F32)
    h = _rmsnorm(x, g_ref[...]).astype(_BF16)

    def up_proj(c):
        return [jnp.dot(h, wup_ref[:, k * f + c * MXU_WIDTH:k * f + (c + 1) * MXU_WIDTH],
                        preferred_element_type=_F32) for k in range(2)]

    for c in range(f // MXU_WIDTH):
        cs = slice(c * MXU_WIDTH, (c + 1) * MXU_WIDTH)
        gp, up = up_proj(c)
        gc = _causal_conv3(gp, carry_ref[:, cs], cw_ref[:, cs]) + cb_ref[:, cs]
        carry_ref[:, cs] = gp[tm - SUBLANES:, :]
        act_ref[:, cs] = (gc * jax.nn.sigmoid(gc) * up).astype(_BF16)
    acc = x + jnp.dot(act_ref[...], wdn_ref[...], preferred_element_type=_F32)
    if has_final:
        acc = _rmsnorm(acc, gf_ref[...])
    o_ref[...] = acc


def _ffn(x, g, w_up, conv_w, conv_b, w_down, layer, attn=None, w_o=None, o_layer=None,
         g_final=None):
    s, d = x.shape
    f = w_down.shape[1]
    tm = min(ROW_TILE, s)
    row_spec = pl.BlockSpec((tm, d), lambda i: (i, 0))
    args, specs = [x], [row_spec]
    if attn is not None:
        args += [attn, w_o]
        specs += [row_spec, _resident(w_o.shape, o_layer)]
    args += [g, w_up, conv_w, conv_b, w_down]
    specs += [_resident((1, d)), _resident(w_up.shape, layer), _resident(conv_w.shape, layer),
              _resident(conv_b.shape, layer), _resident(w_down.shape, layer)]
    if g_final is not None:
        args.append(g_final)
        specs.append(_resident((1, d)))
    return pl.pallas_call(
        functools.partial(_ffn_kernel, has_proj=attn is not None,
                          has_final=g_final is not None),
        grid=(s // tm,),
        in_specs=specs,
        out_specs=row_spec,
        out_shape=jax.ShapeDtypeStruct((s, d), _F32),
        scratch_shapes=[pltpu.VMEM((SUBLANES, f), _F32), pltpu.VMEM((tm, f), _BF16)],
        compiler_params=_params(1),
        name="conv_ffn",
    )(*args)


def _kv_kernel(x_ref, g_ref, wkv_ref, kt_ref, v_ref, km_ref):
    d = x_ref.shape[1]
    nb = kt_ref.shape[0]
    h = _rmsnorm(x_ref[...], g_ref[...]).astype(_BF16)
    k = jnp.dot(h, wkv_ref[:, :d], preferred_element_type=_F32)
    v = jnp.dot(h, wkv_ref[:, d:], preferred_element_type=_F32)
    v_ref[...] = v.astype(_BF16)
    means = []
    for b in range(nb):
        kb = k[b * BLOCK_SIZE:(b + 1) * BLOCK_SIZE, :]
        means.append(jnp.mean(kb, axis=0, keepdims=True))
        kt_ref[b] = kb.T.astype(_BF16)
    km_ref[...] = jnp.concatenate(means, axis=0)


def _shared_kv(x, g, w_kv):
    s, d = x.shape
    tm = min(KV_ROW_TILE, s)
    nb_step = tm // BLOCK_SIZE
    nb = s // BLOCK_SIZE
    return pl.pallas_call(
        _kv_kernel,
        grid=(s // tm,),
        in_specs=[
            pl.BlockSpec((tm, d), lambda i: (i, 0)),
            _resident((1, d)),
            _resident(w_kv.shape),
        ],
        out_specs=[
            pl.BlockSpec((nb_step, d, BLOCK_SIZE), lambda i: (i, 0, 0)),
            pl.BlockSpec((tm, d), lambda i: (i, 0)),
            pl.BlockSpec((nb_step, d), lambda i: (i, 0)),
        ],
        out_shape=[
            jax.ShapeDtypeStruct((nb, d, BLOCK_SIZE), _BF16),
            jax.ShapeDtypeStruct((s, d), _BF16),
            jax.ShapeDtypeStruct((nb, d), _F32),
        ],
        compiler_params=_params(1),
        name="shared_kv",
    )(x, g, w_kv)


def _bf16_pieces(x):
    out = []
    for _ in range(SPLIT):
        piece = float(np.float32(x).astype(_BF16))
        out.append(piece)
        x -= piece
    return out


def _q_gate_kernel(x_ref, g_ref, wq_ref, km_ref, o_ref):
    tm, d = x_ref.shape
    n_heads = d // HEAD_DIM
    scale = HEAD_DIM ** -0.5
    h = _rmsnorm(x_ref[...], g_ref[...]).astype(_BF16)
    q = jnp.dot(h, wq_ref[...], preferred_element_type=_F32)

    nbl = MASK_LANES
    pos = pl.program_id(0) * tm + lax.broadcasted_iota(jnp.int32, (1, tm), 1)
    own = lax.shift_right_logical(pos, BLOCK_SIZE.bit_length() - 1)
    blk = lax.broadcasted_iota(jnp.int32, (nbl, tm), 0)
    valid = blk < own
    tail_row = lax.broadcasted_iota(jnp.int32, (LANES - nbl, tm), 0)

    for hd in range(n_heads):
        qh = q[:, hd * HEAD_DIM:(hd + 1) * HEAD_DIM]
        kmh = km_ref[hd]
        k_hi = kmh.astype(_BF16)
        k_lo = (kmh - k_hi.astype(_F32)).astype(_BF16)
        q_hi = qh.astype(_BF16)
        q_lo = (qh - q_hi.astype(_F32)).astype(_BF16)
        nt = (((1,), (1,)), ((), ()))
        gate = (lax.dot_general(k_hi, q_hi, nt, preferred_element_type=_F32)
                + lax.dot_general(k_hi, q_lo, nt, preferred_element_type=_F32)
                + lax.dot_general(k_lo, q_hi, nt, preferred_element_type=_F32))
        gate = jnp.where(valid, gate, -jnp.inf)
        keep = blk == own
        for _ in range(TOP_K_BLOCKS):
            m = jnp.max(gate, axis=0, keepdims=True)
            first = jnp.min(jnp.where(gate == m, blk, nbl), axis=0, keepdims=True)
            pick = (blk == first) & (m > -jnp.inf)
            keep = keep | pick
            gate = jnp.where(pick, -jnp.inf, gate)
        bias = jnp.where(keep, 0.0, MASK_VALUE)
        slope = 2.0 ** (-8.0 * (hd + 1) / n_heads)
        tail = jnp.zeros((LANES - nbl, tm), _F32)
        for k, piece in enumerate(_bf16_pieces(slope * LOG2E)):
            tail = jnp.where(tail_row == k, piece * BLOCK_SIZE, tail)
            tail = jnp.where(tail_row == SPLIT + k, piece, tail)
        aux = jnp.concatenate([bias, tail], axis=0).T
        o_ref[:, hd * AUG:hd * AUG + HEAD_DIM] = (qh * (scale * LOG2E)).astype(_BF16)
        o_ref[:, hd * AUG + HEAD_DIM:(hd + 1) * AUG] = aux.astype(_BF16)


def _q_gate(x, g, w_q, km, layer):
    s, d = x.shape
    n_heads = d // HEAD_DIM
    tm = min(ROW_TILE, s)
    return pl.pallas_call(
        _q_gate_kernel,
        grid=(s // tm,),
        in_specs=[
            pl.BlockSpec((tm, d), lambda i: (i, 0)),
            _resident((1, d)),
            _resident(w_q.shape, layer),
            _resident(km.shape),
        ],
        out_specs=pl.BlockSpec((tm, n_heads * AUG), lambda i: (i, 0)),
        out_shape=jax.ShapeDtypeStruct((s, n_heads * AUG), _BF16),
        compiler_params=_params(1),
        name="q_gate",
    )(x, g, w_q, km)


def _attn_kernel(q_ref, kt_ref, v_ref, o_ref, kaux_ref, m_ref, alpha_ref, acc_ref, p_ref,
                 *, kb):
    tq = q_ref.shape[0]
    nb = kt_ref.shape[0]
    g = tq // BLOCK_SIZE
    i = pl.program_id(1)

    @pl.when((pl.program_id(0) == 0) & (i == 0))
    def _():
        nbl = MASK_LANES
        row = lax.broadcasted_iota(jnp.int32, (HEAD_DIM, BLOCK_SIZE), 0)
        off = lax.broadcasted_iota(jnp.int32, (HEAD_DIM, BLOCK_SIZE), 1).astype(_F32)
        is_off = (row >= nbl + SPLIT) & (row < nbl + 2 * SPLIT)
        is_id = (row >= nbl) & (row < nbl + SPLIT)
        base = jnp.where(is_off, off, 0.0)
        for n in range(nb):
            kaux_ref[n] = jnp.where(row == n, 1.0,
                                    jnp.where(is_id, float(n), base)).astype(_BF16)

    lane = lax.broadcasted_iota(jnp.int32, (BLOCK_SIZE, HEAD_DIM), 1)
    ones_col = jnp.where(lane == 0, 1.0, 0.0).astype(_BF16)

    m_ref[...] = jnp.full_like(m_ref, MASK_VALUE)
    acc_ref[...] = jnp.zeros_like(acc_ref)

    def scores(r, blocks):
        q = q_ref[r * BLOCK_SIZE:(r + 1) * BLOCK_SIZE, :]
        return [jnp.dot(q, jnp.concatenate([kt_ref[n], kaux_ref[n]], axis=0),
                        preferred_element_type=_F32) for n in blocks]

    def softmax(r, parts, causal_last):
        rows = slice(r * BLOCK_SIZE, (r + 1) * BLOCK_SIZE)
        if causal_last:
            qpos = lax.broadcasted_iota(jnp.int32, (BLOCK_SIZE, BLOCK_SIZE), 0)
            kpos = lax.broadcasted_iota(jnp.int32, (BLOCK_SIZE, BLOCK_SIZE), 1)
            parts[-1] = jnp.where(kpos <= qpos, parts[-1], MASK_VALUE)
        s = parts[0] if len(parts) == 1 else jnp.concatenate(parts, axis=1)
        m_prev = m_ref[rows, :]
        m_new = jnp.maximum(m_prev, jnp.max(s, axis=-1, keepdims=True))
        alpha_ref[rows, :] = jnp.exp2(m_prev - m_new)
        p_ref[rows, :s.shape[1]] = jnp.exp2(s - m_new).astype(_BF16)
        m_ref[rows, :] = m_new

    def pv(r, blocks):
        rows = slice(r * BLOCK_SIZE, (r + 1) * BLOCK_SIZE)
        out = None
        for j, n in enumerate(blocks):
            v_aug = jnp.concatenate([v_ref[n], ones_col], axis=1)
            t = jnp.dot(p_ref[rows, j * BLOCK_SIZE:(j + 1) * BLOCK_SIZE], v_aug,
                        preferred_element_type=_F32)
            out = t if out is None else out + t
        acc_ref[rows, :] = alpha_ref[rows, :] * acc_ref[rows, :] + out

    lag = min(PIPELINE_LAG, g - 1)

    def stage(new_of, causal_last, old_of, descending=False):
        order = list(range(g))[::-1] if descending else list(range(g))
        for k, r in enumerate(order):
            parts = scores(r, new_of(r))
            if k < lag and old_of is not None:
                u = r if descending else g - lag + k
                pv(u, old_of(u))
            softmax(r, parts, causal_last)
        for r in order[:g - lag]:
            pv(r, new_of(r))
        return order[g - lag:]

    def past(t):
        return lambda r: [t * kb + j for j in range(kb)]

    diag = lambda r: [i * g + j for j in range(r + 1)]
    n_past = (i * g) // kb

    @pl.when(n_past == 0)
    def _():
        stage(diag, True, None, descending=True)

    @pl.when(n_past > 0)
    def _():
        stage(past(0), False, None)

        def body(t, carry):
            stage(past(t), False, past(t - 1))
            return carry

        lax.fori_loop(1, n_past, body, 0)
        stage(diag, True, past(n_past - 1), descending=True)

    for r in list(range(g))[::-1][g - lag:]:
        pv(r, diag(r))

    acc = acc_ref[...]
    o_ref[...] = (acc[:, :HEAD_DIM] / acc[:, HEAD_DIM:HEAD_DIM + 1]).astype(o_ref.dtype)


def _attention(q_aug, kt, v):
    s = q_aug.shape[0]
    nb, d, _ = kt.shape
    n_heads = d // HEAD_DIM
    tq = min(Q_TILE, s)
    g = tq // BLOCK_SIZE
    kb = min(KV_BLOCKS_PER_STEP, g)
    assert g % kb == 0 and g >= 2
    v3 = v.reshape(nb, BLOCK_SIZE, d)
    return pl.pallas_call(
        functools.partial(_attn_kernel, kb=kb),
        grid=(n_heads, s // tq),
        in_specs=[
            pl.BlockSpec((tq, AUG), lambda h, i: (i, h)),
            pl.BlockSpec((nb, HEAD_DIM, BLOCK_SIZE), lambda h, i: (0, h, 0)),
            pl.BlockSpec((nb, BLOCK_SIZE, HEAD_DIM), lambda h, i: (0, 0, h)),
        ],
        out_specs=pl.BlockSpec((tq, HEAD_DIM), lambda h, i: (i, h)),
        out_shape=jax.ShapeDtypeStruct((s, d), _BF16),
        scratch_shapes=[
            pltpu.VMEM((nb, HEAD_DIM, BLOCK_SIZE), _BF16),
            pltpu.VMEM((tq, 1), _F32),
            pltpu.VMEM((tq, 1), _F32),
            pltpu.VMEM((tq, AUG), _F32),
            pltpu.VMEM((tq, max(kb, g) * BLOCK_SIZE), _BF16),
        ],
        compiler_params=_params(2),
        name="moba_attention",
    )(q_aug, kt, v3)


def kernel(x, norm_mix, norm_ffn, ffn_w_up, ffn_conv, ffn_conv_b, ffn_w_down,
           a_w_in, a_conv, a_w_out, kv_norm, w_kv, b_w_q, b_w_o, final_norm):
    b, s, d = x.shape
    assert b == 1 and s % BLOCK_SIZE == 0 and d % MXU_WIDTH == 0
    depth = norm_mix.shape[0]
    n_a = a_w_in.shape[0]
    n_heads = d // HEAD_DIM
    nb = s // BLOCK_SIZE
    assert nb <= MASK_LANES, "one mask lane per key block"

    bf = lambda w: w.astype(_BF16)
    ffn_w_up, ffn_w_down = bf(ffn_w_up), bf(ffn_w_down)
    a_w_in, a_w_out = bf(a_w_in), bf(a_w_out)
    w_kv, b_w_q, b_w_o = bf(w_kv), bf(b_w_q), bf(b_w_o)

    conv_b = ffn_conv_b.reshape(depth, 1, -1)
    xs = x.reshape(s, d)
    kt = v = km = None
    for i in range(depth):
        gmix = norm_mix[i].reshape(1, d)
        attn = j = None
        if i < n_a:
            xs = _mixer_a(xs, gmix, a_w_in, a_conv, a_w_out, i)
        else:
            j = i - n_a
            attn = _attention(_q_gate(xs, gmix, b_w_q, km, j), kt, v)
        xs = _ffn(xs, norm_ffn[i].reshape(1, d), ffn_w_up, ffn_conv, conv_b, ffn_w_down, i,
                  attn=attn, w_o=b_w_o if attn is not None else None, o_layer=j,
                  g_final=final_norm.reshape(1, d) if i == depth - 1 else None)
        if i == n_a - 1:
            kt, v, km = _shared_kv(xs, kv_norm.reshape(1, d), w_kv)
            km = km.reshape(nb, n_heads, HEAD_DIM).transpose(1, 0, 2)
            km = jnp.pad(km, ((0, 0), (0, MASK_LANES - nb), (0, 0)))
    return xs.reshape(b, s, d)
```
